```python
import math
import jax, jax.numpy as jnp
from jax import lax
import numpy as np

D_MODEL = 2048
BATCH = 1
SEQ = 8192
DEPTH = 4

N_MIXERS = 2
GRID_W = 64
HEAD_DIM = 128
N_HEADS = D_MODEL // HEAD_DIM
N_KV_HEADS = 4
GROUP = N_HEADS // N_KV_HEADS
Q_DIM = N_HEADS * HEAD_DIM
KV_DIM = N_KV_HEADS * HEAD_DIM
QKV_DIM = Q_DIM + 2 * KV_DIM
Q_BLOCK = 128
ROPE_THETA = 10000.0
ROPE_FREQS = HEAD_DIM // 4
D_RNN = D_MODEL
RNN_BLOCK = 128
N_RNN_BLOCKS = D_RNN // RNN_BLOCK
CONV_W = 4
CONV_PAD_L = 2
C_DECAY = 8.0
D_FF = 5632
NORM_EPS = 1e-6

kernel_name = "hybrid_gqa2drope_bidir_rglru_macaron"


def rms_norm(x, g):
    xf = x.astype(jnp.float32)
    y = xf * lax.rsqrt(jnp.mean(xf * xf, axis=-1, keepdims=True) + NORM_EPS)
    return (y * g.astype(jnp.float32)).astype(x.dtype)


def swiglu(xn, w_gu, w_down):
    gu = xn @ w_gu
    g, u = jnp.split(gu, 2, axis=-1)
    return (jax.nn.silu(g) * u) @ w_down


def axial_angles(S):
    rows_n = S // GRID_W
    rows = jnp.repeat(jnp.arange(rows_n, dtype=jnp.float32), GRID_W)
    cols = jnp.tile(jnp.arange(GRID_W, dtype=jnp.float32), rows_n)
    inv_freq = ROPE_THETA ** (-jnp.arange(ROPE_FREQS, dtype=jnp.float32) / ROPE_FREQS)
    return rows[:, None] * inv_freq[None, :], cols[:, None] * inv_freq[None, :]


def rope_half(xh, ang):
    x1, x2 = jnp.split(xh, 2, axis=-1)
    c, s = jnp.cos(ang), jnp.sin(ang)
    return jnp.concatenate([x1 * c - x2 * s, x2 * c + x1 * s], axis=-1)


def axial_rope(x, ang_row, ang_col):
    xf = x.astype(jnp.float32)
    xr, xc = jnp.split(xf, 2, axis=-1)
    out = jnp.concatenate([rope_half(xr, ang_row), rope_half(xc, ang_col)], axis=-1)
    return out.astype(x.dtype)


def attention_mixer(xn, w_qkv, q_gain, k_gain, w_o):
    B, S, _ = xn.shape
    qkv = xn @ w_qkv
    q = qkv[..., :Q_DIM].reshape(B, S, N_KV_HEADS, GROUP, HEAD_DIM)
    k = qkv[..., Q_DIM:Q_DIM + KV_DIM].reshape(B, S, N_KV_HEADS, HEAD_DIM)
    v = qkv[..., Q_DIM + KV_DIM:].reshape(B, S, N_KV_HEADS, HEAD_DIM)
    q = rms_norm(q, q_gain)
    k = rms_norm(k, k_gain)
    ang_row, ang_col = axial_angles(S)
    q = axial_rope(q, ang_row[None, :, None, None, :], ang_col[None, :, None, None, :])
    k = axial_rope(k, ang_row[None, :, None, :], ang_col[None, :, None, :])
    scale = HEAD_DIM ** -0.5
    n_qb = S // Q_BLOCK
    qb = q.reshape(B, n_qb, Q_BLOCK, N_KV_HEADS, GROUP, HEAD_DIM).swapaxes(0, 1)

    def one_block(q_blk):
        s = jnp.einsum('bqkgd,bskd->bkgqs', q_blk, k).astype(jnp.float32) * scale
        p = jax.nn.softmax(s, axis=-1).astype(v.dtype)
        return jnp.einsum('bkgqs,bskd->bqkgd', p, v)

    o = lax.map(one_block, qb)
    o = o.swapaxes(0, 1).reshape(B, S, Q_DIM)
    return o @ w_o


def centred_depthwise_conv(x, w, b):
    S = x.shape[1]
    xp = jnp.pad(x, ((0, 0), (CONV_PAD_L, CONV_W - 1 - CONV_PAD_L), (0, 0)))
    y = b
    for tap in range(CONV_W):
        y = y + xp[:, tap:tap + S, :] * w[tap]
    return y


def _lin_combine(e1, e2):
    a1, b1 = e1
    a2, b2 = e2
    return a1 * a2, a2 * b1 + b2


def rg_lru(xc, gate_w, gate_b, lam, reverse):
    B, S, _ = xc.shape
    xf = xc.astype(jnp.float32)
    xb = xf.reshape(B, S, N_RNN_BLOCKS, RNN_BLOCK)
    g = jnp.einsum('bsnc,zncd->zbsnd', xb, gate_w.astype(jnp.float32)).reshape(2, B, S, D_RNN)
    g = g + gate_b.astype(jnp.float32)[:, None, None, :]
    r_gate = jax.nn.sigmoid(g[0])
    i_gate = jax.nn.sigmoid(g[1])
    log_a = -C_DECAY * r_gate * jax.nn.softplus(-lam.astype(jnp.float32))
    a = jnp.exp(log_a)
    b = jnp.sqrt(-jnp.expm1(2.0 * log_a)) * (i_gate * xf)
    _, h = lax.associative_scan(_lin_combine, (a, b), axis=1, reverse=reverse)
    return h


def recurrent_mixer(xn, w_in, conv_w, conv_b, gate_w, gate_b, lam, w_out):
    u = xn @ w_in
    gate_branch, rec_branch = jnp.split(u, 2, axis=-1)
    xc = centred_depthwise_conv(rec_branch, conv_w, conv_b)
    h = (rg_lru(xc, gate_w[0], gate_b[0], lam[0], reverse=False)
         + rg_lru(xc, gate_w[1], gate_b[1], lam[1], reverse=True))
    y = jax.nn.gelu(gate_branch) * h.astype(xn.dtype)
    return y @ w_out


def setup_inputs(seed: int = 0) -> dict:
    key = jax.random.key(seed)
    ks = jax.random.split(key, 20)
    n_attn = len(range(0, DEPTH, N_MIXERS))
    n_rec = DEPTH - n_attn
    f32 = jnp.float32

    def nrm(k, shape, fan_in):
        return jax.random.normal(k, shape, f32) * (fan_in ** -0.5)

    def gain(k, shape):
        return 1.0 + 0.02 * jax.random.normal(k, shape, f32)

    x = jax.random.normal(ks[0], (BATCH, SEQ, D_MODEL), f32)
    ffn_norm = gain(ks[1], (DEPTH, 2, D_MODEL))
    ffn_w_gu = nrm(ks[2], (DEPTH, 2, D_MODEL, 2 * D_FF), D_MODEL)
    ffn_w_down = nrm(ks[3], (DEPTH, 2, D_FF, D_MODEL), D_FF)
    attn_norm = gain(ks[4], (n_attn, D_MODEL))
    attn_w_qkv = nrm(ks[5], (n_attn, D_MODEL, QKV_DIM), D_MODEL)
    attn_q_norm = gain(ks[6], (n_attn, HEAD_DIM))
    attn_k_norm = gain(ks[7], (n_attn, HEAD_DIM))
    attn_w_o = nrm(ks[8], (n_attn, Q_DIM, D_MODEL), Q_DIM)
    rec_norm = gain(ks[9], (n_rec, D_MODEL))
    rec_w_in = nrm(ks[10], (n_rec, D_MODEL, 2 * D_RNN), D_MODEL)
    rec_conv_w = nrm(ks[11], (n_rec, CONV_W, D_RNN), CONV_W)
    rec_conv_b = 0.01 * jax.random.normal(ks[12], (n_rec, D_RNN), f32)
    rec_gate_w = nrm(ks[13], (n_rec, 2, 2, N_RNN_BLOCKS, RNN_BLOCK, RNN_BLOCK), RNN_BLOCK)
    rec_gate_b = 0.01 * jax.random.normal(ks[14], (n_rec, 2, 2, D_RNN), f32)
    a8 = jax.random.uniform(ks[15], (n_rec, 2, D_RNN), f32, minval=0.9, maxval=0.999)
    s = a8 ** (1.0 / C_DECAY)
    rec_lambda = jnp.log(s) - jnp.log1p(-s)
    rec_w_out = nrm(ks[16], (n_rec, D_RNN, D_MODEL), D_RNN)
    final_norm = gain(ks[17], (D_MODEL,))
    return {
        "x": x,
        "ffn_norm": ffn_norm, "ffn_w_gu": ffn_w_gu, "ffn_w_down": ffn_w_down,
        "attn_norm": attn_norm, "attn_w_qkv": attn_w_qkv,
        "attn_q_norm": attn_q_norm, "attn_k_norm": attn_k_norm, "attn_w_o": attn_w_o,
        "rec_norm": rec_norm, "rec_w_in": rec_w_in, "rec_conv_w": rec_conv_w,
        "rec_conv_b": rec_conv_b, "rec_gate_w": rec_gate_w, "rec_gate_b": rec_gate_b,
        "rec_lambda": rec_lambda, "rec_w_out": rec_w_out,
        "final_norm": final_norm,
    }


def reference(x, ffn_norm, ffn_w_gu, ffn_w_down, attn_norm, attn_w_qkv, attn_q_norm,
              attn_k_norm, attn_w_o, rec_norm, rec_w_in, rec_conv_w, rec_conv_b,
              rec_gate_w, rec_gate_b, rec_lambda, rec_w_out, final_norm):
    for i in range(DEPTH):
        x = x + 0.5 * swiglu(rms_norm(x, ffn_norm[i, 0]), ffn_w_gu[i, 0], ffn_w_down[i, 0])
        j = i // N_MIXERS
        if i % N_MIXERS == 0:
            x = x + attention_mixer(rms_norm(x, attn_norm[j]), attn_w_qkv[j],
                                    attn_q_norm[j], attn_k_norm[j], attn_w_o[j])
        else:
            x = x + recurrent_mixer(rms_norm(x, rec_norm[j]), rec_w_in[j], rec_conv_w[j],
                                    rec_conv_b[j], rec_gate_w[j], rec_gate_b[j],
                                    rec_lambda[j], rec_w_out[j])
        x = x + 0.5 * swiglu(rms_norm(x, ffn_norm[i, 1]), ffn_w_gu[i, 1], ffn_w_down[i, 1])
    return rms_norm(x, final_norm)
```

```python
import functools
import math

import jax
import jax.numpy as jnp
import numpy as np
from jax import lax
from jax.experimental import pallas as pl
from jax.experimental.pallas import tpu as pltpu

D_MODEL = 2048
DEPTH = 4
N_MIXERS = 2
GRID_W = 64
HEAD_DIM = 128
N_HEADS = D_MODEL // HEAD_DIM
N_KV_HEADS = 4
GROUP = N_HEADS // N_KV_HEADS
Q_DIM = N_HEADS * HEAD_DIM
KV_DIM = N_KV_HEADS * HEAD_DIM
QKV_DIM = Q_DIM + 2 * KV_DIM
ROPE_THETA = 10000.0
ROPE_FREQS = HEAD_DIM // 4
D_RNN = D_MODEL
RNN_BLOCK = 128
N_RNN_BLOCKS = D_RNN // RNN_BLOCK
CONV_W = 4
CONV_PAD_L = 2
C_DECAY = 8.0
D_FF = 5632
NORM_EPS = 1e-6

V7X_VMEM_BYTES = 64 * 1024 * 1024
V7X_VMEM_USABLE_BYTES = 58 * 1024 * 1024
SUBLANES = 8
LANES = 128

F32 = jnp.float32
BF16 = jnp.bfloat16


def _nbytes(shape, dtype):
    return int(np.prod(shape)) * jnp.dtype(dtype).itemsize


def _compiler_params(semantics, vmem_estimate_bytes):
    limit = min(V7X_VMEM_USABLE_BYTES, max(32 * 1024 * 1024, int(vmem_estimate_bytes)))
    return pltpu.CompilerParams(dimension_semantics=semantics, vmem_limit_bytes=limit)


def _rms_scale(x):
    return lax.rsqrt(jnp.mean(x * x, axis=-1, keepdims=True) + NORM_EPS)


def _rms_norm_kernel(x_ref, g_ref, o_ref):
    x = x_ref[...]
    o_ref[...] = (x * _rms_scale(x) * g_ref[...]).astype(o_ref.dtype)


def _rms_norm(x, gain, out_dtype, block_rows=512):
    s, d = x.shape
    est = 2 * (_nbytes((block_rows, d), F32) + _nbytes((block_rows, d), out_dtype)) \
        + 2 * _nbytes((block_rows, d), F32)
    return pl.pallas_call(
        _rms_norm_kernel,
        grid=(s // block_rows,),
        in_specs=[pl.BlockSpec((block_rows, d), lambda i: (i, 0)),
                  pl.BlockSpec((1, d), lambda i: (0, 0))],
        out_specs=pl.BlockSpec((block_rows, d), lambda i: (i, 0)),
        out_shape=jax.ShapeDtypeStruct((s, d), out_dtype),
        name="rms_norm",
        compiler_params=_compiler_params(("parallel",), est),
    )(x, gain.reshape(1, d))


def _swiglu_epilogue(a, b):
    return ((a / (1.0 + jnp.exp(-a))) * b,)


def _gelu_split_epilogue(a, b):
    c = math.sqrt(2.0 / math.pi)
    cdf = 0.5 * (1.0 + jnp.tanh(c * (a + 0.044715 * (a * a * a))))
    return (a * cdf, b)


def _dual_matmul_kernel(x_ref, wa_ref, wb_ref, *out_refs, epilogue):
    x = x_ref[...]
    a = jnp.dot(x, wa_ref[...], preferred_element_type=F32)
    b = jnp.dot(x, wb_ref[...], preferred_element_type=F32)
    for ref, val in zip(out_refs, epilogue(a, b)):
        ref[...] = val.astype(ref.dtype)


def _dual_matmul(x, w, epilogue, out_dtypes, name, block_rows=1024, block_cols=512):
    s, k = x.shape
    n = w.shape[1] // 2
    n_col_blocks = n // block_cols
    est = 2 * _nbytes((block_rows, k), BF16) + 4 * _nbytes((k, block_cols), BF16) \
        + sum(2 * _nbytes((block_rows, block_cols), dt) for dt in out_dtypes) \
        + 4 * _nbytes((block_rows, block_cols), F32)
    out_spec = pl.BlockSpec((block_rows, block_cols), lambda i, j: (i, j))
    return pl.pallas_call(
        functools.partial(_dual_matmul_kernel, epilogue=epilogue),
        grid=(s // block_rows, n_col_blocks),
        in_specs=[pl.BlockSpec((block_rows, k), lambda i, j: (i, 0)),
                  pl.BlockSpec((k, block_cols), lambda i, j: (0, j)),
                  pl.BlockSpec((k, block_cols), lambda i, j: (0, j + n_col_blocks))],
        out_specs=[out_spec] * len(out_dtypes),
        out_shape=[jax.ShapeDtypeStruct((s, n), dt) for dt in out_dtypes],
        name=name,
        compiler_params=_compiler_params(("parallel", "arbitrary"), est),
    )(x, w, w)


def _proj_residual_norm_kernel(y_ref, w_ref, x_ref, g_ref, xo_ref, xn_ref, *, branch_scale):
    acc = jnp.dot(y_ref[...], w_ref[...], preferred_element_type=F32)
    x_new = x_ref[...] + branch_scale * acc
    xo_ref[...] = x_new
    xn_ref[...] = (x_new * _rms_scale(x_new) * g_ref[...]).astype(xn_ref.dtype)


def _proj_residual_norm(y, w, x, next_gain, branch_scale, norm_dtype, block_rows):
    s, k = y.shape
    d = w.shape[1]
    est = _nbytes((k, d), BF16) + 2 * _nbytes((block_rows, k), BF16) \
        + 4 * _nbytes((block_rows, d), F32) + 2 * _nbytes((block_rows, d), norm_dtype) \
        + 3 * _nbytes((block_rows, d), F32)
    row_spec = lambda cols: pl.BlockSpec((block_rows, cols), lambda i: (i, 0))
    return pl.pallas_call(
        functools.partial(_proj_residual_norm_kernel, branch_scale=branch_scale),
        grid=(s // block_rows,),
        in_specs=[row_spec(k),
                  pl.BlockSpec((k, d), lambda i: (0, 0), pipeline_mode=pl.Buffered(1)),
                  row_spec(d),
                  pl.BlockSpec((1, d), lambda i: (0, 0))],
        out_specs=[row_spec(d), row_spec(d)],
        out_shape=[jax.ShapeDtypeStruct((s, d), F32), jax.ShapeDtypeStruct((s, d), norm_dtype)],
        name=f"proj_residual_norm_k{k}",
        compiler_params=_compiler_params(("parallel",), est),
    )(y, w, x, next_gain.reshape(1, d))


QKV_BLOCK_COLS = 512
QK_COL_BLOCKS = (Q_DIM + KV_DIM) // QKV_BLOCK_COLS


def _swap_rotary_halves(x):
    lane = lax.broadcasted_iota(jnp.int32, x.shape, 1)
    first_quarter = (lane % (2 * ROPE_FREQS)) < ROPE_FREQS
    return jnp.where(first_quarter,
                     pltpu.roll(x, HEAD_DIM - ROPE_FREQS, 1),
                     pltpu.roll(x, ROPE_FREQS, 1))


def _qkv_kernel(x_ref, w_ref, gain_ref, cos_ref, sin_ref, o_ref):
    j = pl.program_id(1)
    acc = jnp.dot(x_ref[...], w_ref[...], preferred_element_type=F32)

    @pl.when(j < QK_COL_BLOCKS)
    def _():
        cos_t = cos_ref[...]
        sin_t = sin_ref[...]
        for h in range(QKV_BLOCK_COLS // HEAD_DIM):
            cols = slice(h * HEAD_DIM, (h + 1) * HEAD_DIM)
            xh = acc[:, cols]
            xh = xh * _rms_scale(xh) * gain_ref[:, cols]
            xh = xh * cos_t + _swap_rotary_halves(xh) * sin_t
            o_ref[:, cols] = xh.astype(o_ref.dtype)

    @pl.when(j >= QK_COL_BLOCKS)
    def _():
        o_ref[...] = acc.astype(o_ref.dtype)


def _qkv_proj(xn, w_qkv, head_gain_cols, cos_t, sin_t, block_rows=1024):
    s, k = xn.shape
    est = 2 * _nbytes((block_rows, k), BF16) + 2 * _nbytes((k, QKV_BLOCK_COLS), BF16) \
        + 4 * _nbytes((block_rows, HEAD_DIM), F32) + 2 * _nbytes((block_rows, QKV_BLOCK_COLS), BF16) \
        + 4 * _nbytes((block_rows, QKV_BLOCK_COLS), F32)
    return pl.pallas_call(
        _qkv_kernel,
        grid=(s // block_rows, QKV_DIM // QKV_BLOCK_COLS),
        in_specs=[pl.BlockSpec((block_rows, k), lambda i, j: (i, 0)),
                  pl.BlockSpec((k, QKV_BLOCK_COLS), lambda i, j: (0, j)),
                  pl.BlockSpec((1, QKV_BLOCK_COLS), lambda i, j: (0, j)),
                  pl.BlockSpec((block_rows, HEAD_DIM), lambda i, j: (i, 0)),
                  pl.BlockSpec((block_rows, HEAD_DIM), lambda i, j: (i, 0))],
        out_specs=pl.BlockSpec((block_rows, QKV_BLOCK_COLS), lambda i, j: (i, j)),
        out_shape=jax.ShapeDtypeStruct((s, QKV_DIM), BF16),
        name="qkv_proj",
        compiler_params=_compiler_params(("parallel", "arbitrary"), est),
    )(xn, w_qkv, head_gain_cols, cos_t, sin_t)


def _rotary_tables(s):
    rows = jnp.repeat(jnp.arange(s // GRID_W, dtype=F32), GRID_W)
    cols = jnp.tile(jnp.arange(GRID_W, dtype=F32), s // GRID_W)
    inv_freq = ROPE_THETA ** (-jnp.arange(ROPE_FREQS, dtype=F32) / ROPE_FREQS)
    ang_row = rows[:, None] * inv_freq[None, :]
    ang_col = cols[:, None] * inv_freq[None, :]
    cos_t = jnp.concatenate([jnp.cos(ang_row)] * 2 + [jnp.cos(ang_col)] * 2, axis=-1)
    sin_t = jnp.concatenate([-jnp.sin(ang_row), jnp.sin(ang_row),
                             -jnp.sin(ang_col), jnp.sin(ang_col)], axis=-1)
    return cos_t, sin_t


def _flash_kernel(q_ref, k_ref, v_ref, o_ref, *, kv_chunk):
    tq = q_ref.shape[0]
    s_len = k_ref.shape[0]
    q = jnp.concatenate([q_ref[:, g * HEAD_DIM:(g + 1) * HEAD_DIM] for g in range(GROUP)], axis=0)
    rows = GROUP * tq

    def body(c, carry):
        m, l, acc = carry
        start = pl.multiple_of(c * kv_chunk, kv_chunk)
        k_c = k_ref[pl.ds(start, kv_chunk), :]
        v_c = v_ref[pl.ds(start, kv_chunk), :]
        s = lax.dot_general(q, k_c, (((1,), (1,)), ((), ())), preferred_element_type=F32)
        m_new = jnp.maximum(m, jnp.max(s, axis=-1, keepdims=True))
        alpha = jnp.exp(m - m_new)
        p = jnp.exp(s - m_new)
        l = alpha * l + jnp.sum(p, axis=-1, keepdims=True)
        acc = alpha * acc + jnp.dot(p.astype(BF16), v_c, preferred_element_type=F32)
        return m_new, l, acc

    init = (jnp.full((rows, 1), -jnp.inf, F32), jnp.zeros((rows, 1), F32),
            jnp.zeros((rows, HEAD_DIM), F32))
    _, l, acc = lax.fori_loop(0, s_len // kv_chunk, body, init)
    o = acc / l
    for g in range(GROUP):
        o_ref[:, g * HEAD_DIM:(g + 1) * HEAD_DIM] = o[g * tq:(g + 1) * tq].astype(o_ref.dtype)


def _flash_attention(qkv, block_q=256, kv_chunk=512):
    s = qkv.shape[0]
    group_cols = GROUP * HEAD_DIM
    k_block0 = Q_DIM // HEAD_DIM
    v_block0 = (Q_DIM + KV_DIM) // HEAD_DIM
    rows = GROUP * block_q
    est = 4 * _nbytes((block_q, group_cols), BF16) + 4 * _nbytes((s, HEAD_DIM), BF16) \
        + 4 * _nbytes((rows, kv_chunk), F32) + 6 * _nbytes((rows, HEAD_DIM), F32)
    return pl.pallas_call(
        functools.partial(_flash_kernel, kv_chunk=kv_chunk),
        grid=(N_KV_HEADS, s // block_q),
        in_specs=[pl.BlockSpec((block_q, group_cols), lambda h, i: (i, h)),
                  pl.BlockSpec((s, HEAD_DIM), lambda h, i: (0, k_block0 + h)),
                  pl.BlockSpec((s, HEAD_DIM), lambda h, i: (0, v_block0 + h))],
        out_specs=pl.BlockSpec((block_q, group_cols), lambda h, i: (i, h)),
        out_shape=jax.ShapeDtypeStruct((s, Q_DIM), BF16),
        name="flash_attention",
        compiler_params=_compiler_params(("parallel", "parallel"), est),
    )(qkv, qkv, qkv)


RGLRU_CHUNK = 512
SCAN_UNROLL = 8
CONV_HALO = SUBLANES


def _scan8(a, b, reverse):
    row = lax.broadcasted_iota(jnp.int32, a.shape, 0)
    for d in (1, 2, 4):
        shift = (SUBLANES - d) if reverse else d
        a_prev = pltpu.roll(a, shift, 0)
        b_prev = pltpu.roll(b, shift, 0)
        valid = (row < SUBLANES - d) if reverse else (row >= d)
        b = jnp.where(valid, a * b_prev + b, b)
        a = jnp.where(valid, a * a_prev, a)
    return a, b


def _rglru_kernel(rec_ref, gate_ref, cw_ref, cb_ref, gw_ref, gb_ref, lam_ref, y_ref,
                  xpad_ref, af_ref, bf_ref, ab_ref, bb_ref):
    s_len = rec_ref.shape[0]
    n_chunks = s_len // RGLRU_CHUNK

    zeros_halo = jnp.zeros((CONV_HALO, LANES), F32)
    xpad_ref[0:CONV_HALO, :] = zeros_halo
    xpad_ref[CONV_HALO + s_len:CONV_HALO + s_len + CONV_HALO, :] = zeros_halo
    xpad_ref[CONV_HALO:CONV_HALO + s_len, :] = rec_ref[...]

    neg_lam = -lam_ref[...]
    softplus = jnp.maximum(neg_lam, 0.0) + jnp.log(1.0 + jnp.exp(-jnp.abs(neg_lam)))
    decay_f = -C_DECAY * softplus[0:1, :]
    decay_b = -C_DECAY * softplus[1:2, :]

    def gates_body(c, _):
        start = pl.multiple_of(c * RGLRU_CHUNK, RGLRU_CHUNK)
        window = xpad_ref[pl.ds(start, RGLRU_CHUNK + 2 * CONV_HALO), :]
        xc = cb_ref[...]
        for tap in range(CONV_W):
            off = CONV_HALO - CONV_PAD_L + tap
            xc = xc + window[off:off + RGLRU_CHUNK, :] * cw_ref[tap:tap + 1, :]
        g = jnp.dot(xc.astype(BF16), gw_ref[...], preferred_element_type=F32) + gb_ref[...]
        sig = 1.0 / (1.0 + jnp.exp(-g))
        for direction, (decay, a_ref, b_ref) in enumerate(
                ((decay_f, af_ref, bf_ref), (decay_b, ab_ref, bb_ref))):
            r_gate = sig[:, (2 * direction) * LANES:(2 * direction + 1) * LANES]
            i_gate = sig[:, (2 * direction + 1) * LANES:(2 * direction + 2) * LANES]
            log_a = decay * r_gate
            a = jnp.exp(log_a)
            a_ref[pl.ds(start, RGLRU_CHUNK), :] = a
            b_ref[pl.ds(start, RGLRU_CHUNK), :] = jnp.sqrt(1.0 - a * a) * (i_gate * xc)
        return 0

    lax.fori_loop(0, n_chunks, gates_body, 0)

    n_groups = s_len // SUBLANES
    n_iters = n_groups // SCAN_UNROLL

    def scan_body(it, carry):
        h_f, h_b = carry
        for u in range(SCAN_UNROLL):
            gf = it * SCAN_UNROLL + u
            start_f = pl.multiple_of(gf * SUBLANES, SUBLANES)
            a8, b8 = _scan8(af_ref[pl.ds(start_f, SUBLANES), :], bf_ref[pl.ds(start_f, SUBLANES), :],
                            reverse=False)
            hf8 = a8 * h_f + b8
            bf_ref[pl.ds(start_f, SUBLANES), :] = hf8
            h_f = hf8[SUBLANES - 1:SUBLANES, :]

            start_b = pl.multiple_of((n_groups - 1 - gf) * SUBLANES, SUBLANES)
            a8, b8 = _scan8(ab_ref[pl.ds(start_b, SUBLANES), :], bb_ref[pl.ds(start_b, SUBLANES), :],
                            reverse=True)
            hb8 = a8 * h_b + b8
            bb_ref[pl.ds(start_b, SUBLANES), :] = hb8
            h_b = hb8[0:1, :]
        return h_f, h_b

    zero_state = jnp.zeros((1, LANES), F32)
    lax.fori_loop(0, n_iters, scan_body, (zero_state, zero_state))

    def out_body(c, _):
        start = pl.multiple_of(c * RGLRU_CHUNK, RGLRU_CHUNK)
        rows = pl.ds(start, RGLRU_CHUNK)
        y_ref[rows, :] = (gate_ref[rows, :] * (bf_ref[rows, :] + bb_ref[rows, :])).astype(y_ref.dtype)
        return 0

    lax.fori_loop(0, n_chunks, out_body, 0)


def _rglru(rec, gelu_gate, conv_w, conv_b, gate_w_cat, gate_b_cat, lam):
    s = rec.shape[0]
    col_spec = lambda rows: pl.BlockSpec((rows, LANES), lambda j: (0, j))
    seq_f32 = _nbytes((s, LANES), F32)
    est = 4 * seq_f32 + 2 * _nbytes((s, LANES), BF16) + 5 * seq_f32 \
        + 12 * _nbytes((RGLRU_CHUNK, 4 * LANES), F32)
    return pl.pallas_call(
        _rglru_kernel,
        grid=(N_RNN_BLOCKS,),
        in_specs=[col_spec(s), col_spec(s), col_spec(CONV_W), col_spec(1),
                  pl.BlockSpec((None, RNN_BLOCK, 4 * RNN_BLOCK), lambda j: (j, 0, 0)),
                  pl.BlockSpec((None, 1, 4 * RNN_BLOCK), lambda j: (j, 0, 0)),
                  col_spec(2)],
        out_specs=col_spec(s),
        out_shape=jax.ShapeDtypeStruct((s, D_RNN), BF16),
        scratch_shapes=[pltpu.VMEM((s + 2 * CONV_HALO, LANES), F32)]
        + [pltpu.VMEM((s, LANES), F32)] * 4,
        name="rglru",
        compiler_params=_compiler_params(("parallel",), est),
    )(rec, gelu_gate, conv_w, conv_b.reshape(1, D_RNN), gate_w_cat, gate_b_cat, lam)


def _ffn(x, xn, w_gu, w_down, next_gain, norm_dtype):
    (h,) = _dual_matmul(xn, w_gu.astype(BF16), _swiglu_epilogue, (BF16,), "ffn_gate_up")
    return _proj_residual_norm(h, w_down.astype(BF16), x, next_gain, 0.5, norm_dtype, block_rows=256)


def kernel(x, ffn_norm, ffn_w_gu, ffn_w_down, attn_norm, attn_w_qkv, attn_q_norm, attn_k_norm,
           attn_w_o, rec_norm, rec_w_in, rec_conv_w, rec_conv_b, rec_gate_w, rec_gate_b,
           rec_lambda, rec_w_out, final_norm):
    b, s, d = x.shape
    assert (b, d) == (1, D_MODEL) and s % 1024 == 0
    x = x.reshape(s, d)
    cos_t, sin_t = _rotary_tables(s)
    xn = _rms_norm(x, ffn_norm[0, 0], BF16)
    for i in range(DEPTH):
        j = i // N_MIXERS
        is_attn = i % N_MIXERS == 0
        mixer_gain = attn_norm[j] if is_attn else rec_norm[j]
        x, xn = _ffn(x, xn, ffn_w_gu[i, 0], ffn_w_down[i, 0], mixer_gain, BF16)
        if is_attn:
            softmax_scale = HEAD_DIM ** -0.5
            head_gain_cols = jnp.concatenate(
                [jnp.tile(attn_q_norm[j] * softmax_scale, N_HEADS),
                 jnp.tile(attn_k_norm[j], N_KV_HEADS),
                 jnp.ones((KV_DIM,), F32)]).reshape(1, QKV_DIM)
            qkv = _qkv_proj(xn, attn_w_qkv[j].astype(BF16), head_gain_cols, cos_t, sin_t)
            y = _flash_attention(qkv)
            w_mix_out = attn_w_o[j]
        else:
            gelu_gate, rec = _dual_matmul(xn, rec_w_in[j].astype(BF16), _gelu_split_epilogue,
                                          (F32, F32), "rec_in_proj")
            gate_w_cat = jnp.transpose(rec_gate_w[j], (2, 3, 0, 1, 4)).reshape(
                N_RNN_BLOCKS, RNN_BLOCK, 4 * RNN_BLOCK).astype(BF16)
            gate_b_cat = jnp.transpose(
                rec_gate_b[j].reshape(2, 2, N_RNN_BLOCKS, RNN_BLOCK), (2, 0, 1, 3)).reshape(
                N_RNN_BLOCKS, 1, 4 * RNN_BLOCK)
            y = _rglru(rec, gelu_gate, rec_conv_w[j], rec_conv_b[j], gate_w_cat, gate_b_cat,
                       rec_lambda[j])
            w_mix_out = rec_w_out[j]
        x, xn = _proj_residual_norm(y, w_mix_out.astype(BF16), x, ffn_norm[i, 1], 1.0, BF16,
                                    block_rows=512)
        last = i == DEPTH - 1
        next_gain = final_norm if last else ffn_norm[i + 1, 0]
        x, xn = _ffn(x, xn, ffn_w_gu[i, 1], ffn_w_down[i, 1], next_gain, F32 if last else BF16)
    return xn.reshape(b, s, d)
```

```python
import functools
import math

import jax
import jax.numpy as jnp
import numpy as np
from jax import lax
from jax.experimental import pallas as pl
from jax.experimental.pallas import tpu as pltpu

D_MODEL = 2048
DEPTH = 4
N_MIXERS = 2
GRID_W = 64
HEAD_DIM = 128
N_HEADS = D_MODEL // HEAD_DIM
N_KV_HEADS = 4
GROUP = N_HEADS // N_KV_HEADS
Q_DIM = N_HEADS * HEAD_DIM
KV_DIM = N_KV_HEADS * HEAD_DIM
QKV_DIM = Q_DIM + 2 * KV_DIM
ROPE_THETA = 10000.0
ROPE_FREQS = HEAD_DIM // 4
D_RNN = D_MODEL
RNN_BLOCK = 128
N_RNN_BLOCKS = D_RNN // RNN_BLOCK
CONV_W = 4
CONV_PAD_L = 2
C_DECAY = 8.0
D_FF = 5632
NORM_EPS = 1e-6

V7X_VMEM_USABLE_BYTES = 58 * 1024 * 1024
SUBLANES = 8
LANES = 128

F32 = jnp.float32
BF16 = jnp.bfloat16
LOG2_E = math.log2(math.e)


def _nbytes(shape, dtype):
    return int(np.prod(shape)) * jnp.dtype(dtype).itemsize


def _compiler_params(semantics, vmem_estimate_bytes):
    limit = min(V7X_VMEM_USABLE_BYTES, max(32 * 1024 * 1024, int(vmem_estimate_bytes)))
    return pltpu.CompilerParams(dimension_semantics=semantics, vmem_limit_bytes=limit)


def _rms_scale(x):
    return lax.rsqrt(jnp.mean(x * x, axis=-1, keepdims=True) + NORM_EPS)


def _stacked_spec(block_shape, lead, index_map):
    return pl.BlockSpec((None,) * len(lead) + tuple(block_shape),
                        lambda *g: tuple(lead) + tuple(index_map(*g)))


def _rms_norm_kernel(x_ref, g_ref, o_ref):
    x = x_ref[...]
    o_ref[...] = (x * _rms_scale(x) * g_ref[...]).astype(o_ref.dtype)


def _rms_norm(x, gain, out_dtype, block_rows=512):
    s, d = x.shape
    est = 2 * (_nbytes((block_rows, d), F32) + _nbytes((block_rows, d), out_dtype)) \
        + 2 * _nbytes((block_rows, d), F32)
    return pl.pallas_call(
        _rms_norm_kernel,
        grid=(s // block_rows,),
        in_specs=[pl.BlockSpec((block_rows, d), lambda i: (i, 0)),
                  pl.BlockSpec((1, d), lambda i: (0, 0))],
        out_specs=pl.BlockSpec((block_rows, d), lambda i: (i, 0)),
        out_shape=jax.ShapeDtypeStruct((s, d), out_dtype),
        name="rms_norm",
        compiler_params=_compiler_params(("parallel",), est),
    )(x, gain.reshape(1, d))


def _swiglu_epilogue(a, b):
    return ((a / (1.0 + jnp.exp(-a))) * b,)


def _gelu_split_epilogue(a, b):
    c = math.sqrt(2.0 / math.pi)
    cdf = 0.5 * (1.0 + jnp.tanh(c * (a + 0.044715 * (a * a * a))))
    return (a * cdf, b)


def _dual_matmul_kernel(x_ref, wa_ref, wb_ref, *out_refs, epilogue):
    x = x_ref[...]
    a = jnp.dot(x, wa_ref[...].astype(BF16), preferred_element_type=F32)
    b = jnp.dot(x, wb_ref[...].astype(BF16), preferred_element_type=F32)
    for ref, val in zip(out_refs, epilogue(a, b)):
        ref[...] = val.astype(ref.dtype)


def _dual_matmul(x, w, lead, epilogue, out_dtypes, name, block_rows=1024, block_cols=512):
    s, k = x.shape
    n = w.shape[-1] // 2
    n_col_blocks = n // block_cols
    est = 2 * _nbytes((block_rows, k), BF16) + 4 * _nbytes((k, block_cols), w.dtype) \
        + 2 * _nbytes((k, block_cols), BF16) \
        + sum(2 * _nbytes((block_rows, block_cols), dt) for dt in out_dtypes) \
        + 4 * _nbytes((block_rows, block_cols), F32)
    out_spec = pl.BlockSpec((block_rows, block_cols), lambda i, j: (i, j))
    return pl.pallas_call(
        functools.partial(_dual_matmul_kernel, epilogue=epilogue),
        grid=(s // block_rows, n_col_blocks),
        in_specs=[pl.BlockSpec((block_rows, k), lambda i, j: (i, 0)),
                  _stacked_spec((k, block_cols), lead, lambda i, j: (0, j)),
                  _stacked_spec((k, block_cols), lead, lambda i, j: (0, j + n_col_blocks))],
        out_specs=[out_spec] * len(out_dtypes),
        out_shape=[jax.ShapeDtypeStruct((s, n), dt) for dt in out_dtypes],
        name=name,
        compiler_params=_compiler_params(("parallel", "arbitrary"), est),
    )(x, w, w)


def _proj_residual_norm_kernel(y_ref, w_ref, x_ref, g_ref, xo_ref, xn_ref, *, branch_scale):
    acc = jnp.dot(y_ref[...], w_ref[...], preferred_element_type=F32)
    x_new = x_ref[...] + branch_scale * acc
    xo_ref[...] = x_new
    xn_ref[...] = (x_new * _rms_scale(x_new) * g_ref[...]).astype(xn_ref.dtype)


def _proj_residual_norm(y, w, lead, x, next_gain, branch_scale, norm_dtype, block_rows):
    s, k = y.shape
    d = w.shape[-1]
    est = _nbytes((k, d), BF16) + 2 * _nbytes((block_rows, k), BF16) \
        + 4 * _nbytes((block_rows, d), F32) + 2 * _nbytes((block_rows, d), norm_dtype) \
        + 3 * _nbytes((block_rows, d), F32)
    row_spec = lambda cols: pl.BlockSpec((block_rows, cols), lambda i: (i, 0))
    w_spec = pl.BlockSpec((None,) * len(lead) + (k, d), lambda i: tuple(lead) + (0, 0),
                          pipeline_mode=pl.Buffered(1))
    return pl.pallas_call(
        functools.partial(_proj_residual_norm_kernel, branch_scale=branch_scale),
        grid=(s // block_rows,),
        in_specs=[row_spec(k), w_spec, row_spec(d), pl.BlockSpec((1, d), lambda i: (0, 0))],
        out_specs=[row_spec(d), row_spec(d)],
        out_shape=[jax.ShapeDtypeStruct((s, d), F32), jax.ShapeDtypeStruct((s, d), norm_dtype)],
        name=f"proj_residual_norm_k{k}",
        compiler_params=_compiler_params(("parallel",), est),
    )(y, w, x, next_gain.reshape(1, d))


KV_CHUNK = 512
QKV_COLS_PER_DOT = 512


def _swap_rotary_halves(x):
    lane = lax.broadcasted_iota(jnp.int32, x.shape, 1)
    first_quarter = (lane % (2 * ROPE_FREQS)) < ROPE_FREQS
    return jnp.where(first_quarter,
                     pltpu.roll(x, HEAD_DIM - ROPE_FREQS, 1),
                     pltpu.roll(x, ROPE_FREQS, 1))


def _qkv_kernel(x_ref, w_ref, gain_ref, cos_ref, sin_ref, q_ref, kt_ref, v_ref):
    x = x_ref[...]
    cos_t = cos_ref[...]
    sin_t = sin_ref[...]
    heads_per_dot = QKV_COLS_PER_DOT // HEAD_DIM

    def normed_rotated(acc, h, col0):
        cols = slice(col0 + h * HEAD_DIM, col0 + (h + 1) * HEAD_DIM)
        xh = acc[:, h * HEAD_DIM:(h + 1) * HEAD_DIM]
        xh = xh * _rms_scale(xh) * gain_ref[:, cols]
        return xh * cos_t + _swap_rotary_halves(xh) * sin_t

    for jb in range(QKV_DIM // QKV_COLS_PER_DOT):
        col0 = jb * QKV_COLS_PER_DOT
        acc = jnp.dot(x, w_ref[:, col0:col0 + QKV_COLS_PER_DOT], preferred_element_type=F32)
        if col0 < Q_DIM:
            for h in range(heads_per_dot):
                q_ref[:, col0 + h * HEAD_DIM:col0 + (h + 1) * HEAD_DIM] = \
                    normed_rotated(acc, h, col0).astype(q_ref.dtype)
        elif col0 < Q_DIM + KV_DIM:
            for h in range(heads_per_dot):
                row0 = col0 - Q_DIM + h * HEAD_DIM
                kt_ref[row0:row0 + HEAD_DIM, :] = \
                    normed_rotated(acc, h, col0).T.astype(kt_ref.dtype)
        else:
            v0 = col0 - Q_DIM - KV_DIM
            v_ref[:, v0:v0 + QKV_COLS_PER_DOT] = acc.astype(v_ref.dtype)


def _qkv_proj(xn, w_qkv, lead, head_gain_cols, cos_t, sin_t):
    s, k = xn.shape
    rows = KV_CHUNK
    est = _nbytes((k, QKV_DIM), BF16) + 2 * _nbytes((rows, k), BF16) \
        + 4 * _nbytes((rows, HEAD_DIM), F32) + 2 * _nbytes((rows, QKV_DIM), BF16) \
        + 6 * _nbytes((rows, QKV_COLS_PER_DOT), F32)
    return pl.pallas_call(
        _qkv_kernel,
        grid=(s // rows,),
        in_specs=[pl.BlockSpec((rows, k), lambda i: (i, 0)),
                  pl.BlockSpec((None,) * len(lead) + (k, QKV_DIM),
                               lambda i: tuple(lead) + (0, 0), pipeline_mode=pl.Buffered(1)),
                  pl.BlockSpec((1, QKV_DIM), lambda i: (0, 0)),
                  pl.BlockSpec((rows, HEAD_DIM), lambda i: (i, 0)),
                  pl.BlockSpec((rows, HEAD_DIM), lambda i: (i, 0))],
        out_specs=[pl.BlockSpec((rows, Q_DIM), lambda i: (i, 0)),
                   pl.BlockSpec((None, KV_DIM, rows), lambda i: (i, 0, 0)),
                   pl.BlockSpec((rows, KV_DIM), lambda i: (i, 0))],
        out_shape=[jax.ShapeDtypeStruct((s, Q_DIM), BF16),
                   jax.ShapeDtypeStruct((s // rows, KV_DIM, rows), BF16),
                   jax.ShapeDtypeStruct((s, KV_DIM), BF16)],
        name="qkv_proj",
        compiler_params=_compiler_params(("parallel",), est),
    )(xn, w_qkv, head_gain_cols, cos_t, sin_t)


def _rotary_tables(s):
    rows = jnp.repeat(jnp.arange(s // GRID_W, dtype=F32), GRID_W)
    cols = jnp.tile(jnp.arange(GRID_W, dtype=F32), s // GRID_W)
    inv_freq = ROPE_THETA ** (-jnp.arange(ROPE_FREQS, dtype=F32) / ROPE_FREQS)
    ang_row = rows[:, None] * inv_freq[None, :]
    ang_col = cols[:, None] * inv_freq[None, :]
    cos_t = jnp.concatenate([jnp.cos(ang_row)] * 2 + [jnp.cos(ang_col)] * 2, axis=-1)
    sin_t = jnp.concatenate([-jnp.sin(ang_row), jnp.sin(ang_row),
                             -jnp.sin(ang_col), jnp.sin(ang_col)], axis=-1)
    return cos_t, sin_t


def _flash_kernel(q_ref, kt_ref, v_ref, o_ref, q_all, s_ref, m_ref, l_ref, acc_ref):
    tq = q_ref.shape[0]
    n_chunks = kt_ref.shape[0]
    lane_tiles = KV_CHUNK // LANES

    for g in range(GROUP):
        q_all[g * tq:(g + 1) * tq, :] = q_ref[:, g * HEAD_DIM:(g + 1) * HEAD_DIM]

    def scores(c):
        return jnp.dot(q_all[...], kt_ref[c], preferred_element_type=F32)

    def absorb(c, s, m, l, acc):
        tiles = [s[:, t * LANES:(t + 1) * LANES] for t in range(lane_tiles)]
        tile_max = functools.reduce(jnp.maximum, tiles)
        m_new = jnp.maximum(m, jnp.max(tile_max, axis=-1, keepdims=True))
        alpha = jnp.exp2(m - m_new)
        p_tiles = [jnp.exp2(t - m_new) for t in tiles]
        l = alpha * l + functools.reduce(jnp.add, p_tiles)
        p = jnp.concatenate(p_tiles, axis=1).astype(BF16)
        start = pl.multiple_of(c * KV_CHUNK, KV_CHUNK)
        pv = jnp.dot(p, v_ref[pl.ds(start, KV_CHUNK), :], preferred_element_type=F32)
        return m_new, l, alpha * acc + pv

    s_ref[...] = scores(0)
    m_ref[...] = jnp.full(m_ref.shape, -jnp.inf, F32)
    l_ref[...] = jnp.zeros(l_ref.shape, F32)
    acc_ref[...] = jnp.zeros(acc_ref.shape, F32)

    def pair_body(it, _):
        c0 = 2 * it
        s_odd = scores(c0 + 1)
        m, l, acc = absorb(c0, s_ref[...], m_ref[...], l_ref[...], acc_ref[...])
        s_even = scores(jnp.minimum(c0 + 2, n_chunks - 1))
        m, l, acc = absorb(c0 + 1, s_odd, m, l, acc)
        m_ref[...] = m
        l_ref[...] = l
        acc_ref[...] = acc
        s_ref[...] = s_even
        return 0

    lax.fori_loop(0, n_chunks // 2, pair_body, 0)

    for g in range(GROUP):
        r = slice(g * tq, (g + 1) * tq)
        l_row = jnp.sum(l_ref[r, :], axis=-1, keepdims=True)
        o_ref[:, g * HEAD_DIM:(g + 1) * HEAD_DIM] = (acc_ref[r, :] / l_row).astype(o_ref.dtype)


def _flash_attention(q, kt, v, block_q=256):
    s = q.shape[0]
    n_chunks = s // KV_CHUNK
    assert n_chunks % 2 == 0
    group_cols = GROUP * HEAD_DIM
    rows = GROUP * block_q
    stat = pltpu.VMEM((rows, LANES), F32)
    est = 4 * _nbytes((block_q, group_cols), BF16) + 4 * _nbytes((s, HEAD_DIM), BF16) \
        + _nbytes((rows, HEAD_DIM), BF16) + _nbytes((rows, KV_CHUNK), F32) \
        + 3 * _nbytes((rows, LANES), F32) + 4 * _nbytes((rows, KV_CHUNK), F32)
    return pl.pallas_call(
        _flash_kernel,
        grid=(N_KV_HEADS, s // block_q),
        in_specs=[pl.BlockSpec((block_q, group_cols), lambda h, i: (i, h)),
                  pl.BlockSpec((n_chunks, HEAD_DIM, KV_CHUNK), lambda h, i: (0, h, 0)),
                  pl.BlockSpec((s, HEAD_DIM), lambda h, i: (0, h))],
        out_specs=pl.BlockSpec((block_q, group_cols), lambda h, i: (i, h)),
        out_shape=jax.ShapeDtypeStruct((s, Q_DIM), BF16),
        scratch_shapes=[pltpu.VMEM((rows, HEAD_DIM), BF16),
                        pltpu.VMEM((rows, KV_CHUNK), F32), stat, stat, stat],
        name="flash_attention",
        compiler_params=_compiler_params(("parallel", "parallel"), est),
    )(q, kt, v)


RGLRU_CHUNK = 512
SCAN_UNROLL = 8
SCAN_BLOCK = SCAN_UNROLL * SUBLANES
CONV_HALO = SUBLANES


def _scan8(a, b, reverse):
    row = lax.broadcasted_iota(jnp.int32, a.shape, 0)
    for d in (1, 2, 4):
        shift = (SUBLANES - d) if reverse else d
        valid = (row < SUBLANES - d) if reverse else (row >= d)
        a_prev = jnp.where(valid, pltpu.roll(a, shift, 0), 1.0)
        b_prev = jnp.where(valid, pltpu.roll(b, shift, 0), 0.0)
        b = a * b_prev + b
        a = a * a_prev
    return a, b


def _rglru_kernel(rec_ref, gate_ref, cw_ref, cb_ref, gw_ref, gb_ref, lam_ref, y_ref,
                  xpad_ref, af_ref, bf_ref, ab_ref, bb_ref, hb_ref):
    s_len = rec_ref.shape[0]
    n_chunks = s_len // RGLRU_CHUNK

    zeros_halo = jnp.zeros((CONV_HALO, LANES), F32)
    xpad_ref[0:CONV_HALO, :] = zeros_halo
    xpad_ref[CONV_HALO + s_len:CONV_HALO + s_len + CONV_HALO, :] = zeros_halo
    xpad_ref[CONV_HALO:CONV_HALO + s_len, :] = rec_ref[...]

    neg_lam = -lam_ref[...]
    softplus = jnp.maximum(neg_lam, 0.0) + jnp.log(1.0 + jnp.exp(-jnp.abs(neg_lam)))
    half_decay = (-0.5 * C_DECAY * LOG2_E) * softplus

    def gates_body(c, _):
        start = pl.multiple_of(c * RGLRU_CHUNK, RGLRU_CHUNK)
        xc = cb_ref[...]
        for tap in range(CONV_W):
            off = CONV_HALO - CONV_PAD_L + tap
            xc = xc + xpad_ref[pl.ds(start + off, RGLRU_CHUNK), :] * cw_ref[tap:tap + 1, :]
        t = jnp.tanh(jnp.dot(xc.astype(BF16), gw_ref[...], preferred_element_type=F32)
                     + gb_ref[...])
        xc_half = 0.5 * xc
        for direction, (a_ref, b_ref) in enumerate(((af_ref, bf_ref), (ab_ref, bb_ref))):
            t_r = t[:, (2 * direction) * LANES:(2 * direction + 1) * LANES]
            t_i = t[:, (2 * direction + 1) * LANES:(2 * direction + 2) * LANES]
            hd = half_decay[direction:direction + 1, :]
            a = jnp.exp2(hd * t_r + hd)
            a_ref[pl.ds(start, RGLRU_CHUNK), :] = a
            b_ref[pl.ds(start, RGLRU_CHUNK), :] = \
                jnp.sqrt(1.0 - a * a) * (xc_half + xc_half * t_i)
        return 0

    lax.fori_loop(0, n_chunks, gates_body, 0)

    n_iters = s_len // SCAN_BLOCK

    def scan_block(a_ref, b_ref, h_ref, h_row0, start, carry, reverse):
        a_blk = a_ref[pl.ds(start, SCAN_BLOCK), :]
        b_blk = b_ref[pl.ds(start, SCAN_BLOCK), :]
        order = range(SCAN_UNROLL - 1, -1, -1) if reverse else range(SCAN_UNROLL)
        last = 0 if reverse else SUBLANES - 1
        h_groups = [None] * SCAN_UNROLL
        for u in order:
            rows = slice(u * SUBLANES, (u + 1) * SUBLANES)
            a8, b8 = _scan8(a_blk[rows], b_blk[rows], reverse)
            a_out = jnp.broadcast_to(a8[last:last + 1, :], a8.shape)
            b_out = jnp.broadcast_to(b8[last:last + 1, :], b8.shape)
            h_groups[u] = a8 * carry + b8
            carry = a_out * carry + b_out
        h_ref[pl.ds(h_row0 + start, SCAN_BLOCK), :] = jnp.concatenate(h_groups, axis=0)
        return carry

    def scan_body(it, carry):
        h_f, h_b = carry
        start_f = pl.multiple_of(it * SCAN_BLOCK, SCAN_BLOCK)
        start_b = pl.multiple_of((n_iters - 1 - it) * SCAN_BLOCK, SCAN_BLOCK)
        h_f = scan_block(af_ref, bf_ref, xpad_ref, CONV_HALO, start_f, h_f, reverse=False)
        h_b = scan_block(ab_ref, bb_ref, hb_ref, 0, start_b, h_b, reverse=True)
        return h_f, h_b

    zero_state = jnp.zeros((SUBLANES, LANES), F32)
    lax.fori_loop(0, n_iters, scan_body, (zero_state, zero_state))

    def out_body(c, _):
        start = pl.multiple_of(c * RGLRU_CHUNK, RGLRU_CHUNK)
        rows = pl.ds(start, RGLRU_CHUNK)
        h = xpad_ref[pl.ds(CONV_HALO + start, RGLRU_CHUNK), :] + hb_ref[rows, :]
        y_ref[rows, :] = (gate_ref[rows, :] * h).astype(y_ref.dtype)
        return 0

    lax.fori_loop(0, n_chunks, out_body, 0)


def _rglru(rec, gelu_gate, conv_w, conv_b, gate_w_half, gate_b_half, lam):
    s = rec.shape[0]
    col_spec = lambda rows: pl.BlockSpec((rows, LANES), lambda j: (0, j))
    seq_f32 = _nbytes((s, LANES), F32)
    est = 4 * seq_f32 + 2 * _nbytes((s, LANES), BF16) + 6 * seq_f32 \
        + 12 * _nbytes((RGLRU_CHUNK, 4 * LANES), F32)
    return pl.pallas_call(
        _rglru_kernel,
        grid=(N_RNN_BLOCKS,),
        in_specs=[col_spec(s), col_spec(s), col_spec(CONV_W), col_spec(1),
                  pl.BlockSpec((None, RNN_BLOCK, 4 * RNN_BLOCK), lambda j: (j, 0, 0)),
                  pl.BlockSpec((None, 1, 4 * RNN_BLOCK), lambda j: (j, 0, 0)),
                  col_spec(2)],
        out_specs=col_spec(s),
        out_shape=jax.ShapeDtypeStruct((s, D_RNN), BF16),
        scratch_shapes=[pltpu.VMEM((s + 2 * CONV_HALO, LANES), F32)]
        + [pltpu.VMEM((s, LANES), F32)] * 5,
        name="rglru",
        compiler_params=_compiler_params(("parallel",), est),
    )(rec, gelu_gate, conv_w, conv_b.reshape(1, D_RNN), gate_w_half, gate_b_half, lam)


def kernel(x, ffn_norm, ffn_w_gu, ffn_w_down, attn_norm, attn_w_qkv, attn_q_norm, attn_k_norm,
           attn_w_o, rec_norm, rec_w_in, rec_conv_w, rec_conv_b, rec_gate_w, rec_gate_b,
           rec_lambda, rec_w_out, final_norm):
    b, s, d = x.shape
    assert (b, d) == (1, D_MODEL) and s % 1024 == 0
    x = x.reshape(s, d)
    cos_t, sin_t = _rotary_tables(s)
    w_down = ffn_w_down.astype(BF16)
    w_qkv = attn_w_qkv.astype(BF16)
    w_o = attn_w_o.astype(BF16)
    w_out = rec_w_out.astype(BF16)

    def ffn(x, xn, layer, half, next_gain, norm_dtype):
        (h,) = _dual_matmul(xn, ffn_w_gu, (layer, half), _swiglu_epilogue, (BF16,), "ffn_gate_up")
        return _proj_residual_norm(h, w_down, (layer, half), x, next_gain, 0.5, norm_dtype,
                                   block_rows=256)

    xn = _rms_norm(x, ffn_norm[0, 0], BF16)
    for i in range(DEPTH):
        j = i // N_MIXERS
        is_attn = i % N_MIXERS == 0
        x, xn = ffn(x, xn, i, 0, attn_norm[j] if is_attn else rec_norm[j], BF16)
        if is_attn:
            q_scale = (HEAD_DIM ** -0.5) * LOG2_E
            head_gain_cols = jnp.concatenate(
                [jnp.tile(attn_q_norm[j] * q_scale, N_HEADS),
                 jnp.tile(attn_k_norm[j], N_KV_HEADS),
                 jnp.ones((KV_DIM,), F32)]).reshape(1, QKV_DIM)
            q, kt, v = _qkv_proj(xn, w_qkv, (j,), head_gain_cols, cos_t, sin_t)
            y = _flash_attention(q, kt, v)
            w_mix, lead = w_o, (j,)
        else:
            gelu_gate, rec = _dual_matmul(xn, rec_w_in, (j,), _gelu_split_epilogue, (F32, F32),
                                          "rec_in_proj")
            gate_w_half = (0.5 * jnp.transpose(rec_gate_w[j], (2, 3, 0, 1, 4))).reshape(
                N_RNN_BLOCKS, RNN_BLOCK, 4 * RNN_BLOCK).astype(BF16)
            gate_b_half = 0.5 * jnp.transpose(
                rec_gate_b[j].reshape(2, 2, N_RNN_BLOCKS, RNN_BLOCK), (2, 0, 1, 3)).reshape(
                N_RNN_BLOCKS, 1, 4 * RNN_BLOCK)
            y = _rglru(rec, gelu_gate, rec_conv_w[j], rec_conv_b[j], gate_w_half, gate_b_half,
                       rec_lambda[j])
            w_mix, lead = w_out, (j,)
        x, xn = _proj_residual_norm(y, w_mix, lead, x, ffn_norm[i, 1], 1.0, BF16, block_rows=512)
        last = i == DEPTH - 1
        next_gain = final_norm if last else ffn_norm[i + 1, 0]
        x, xn = ffn(x, xn, i, 1, next_gain, F32 if last else BF16)
    return xn.reshape(b, s, d)
```

```python
import functools
import math

import jax
import jax.numpy as jnp
import numpy as np
from jax import lax
from jax.experimental import pallas as pl
from jax.experimental.pallas import tpu as pltpu

D_MODEL = 2048
DEPTH = 4
N_MIXERS = 2
GRID_W = 64
HEAD_DIM = 128
N_HEADS = D_MODEL // HEAD_DIM
N_KV_HEADS = 4
GROUP = N_HEADS // N_KV_HEADS
Q_DIM = N_HEADS * HEAD_DIM
KV_DIM = N_KV_HEADS * HEAD_DIM
QKV_DIM = Q_DIM + 2 * KV_DIM
ROPE_THETA = 10000.0
ROPE_FREQS = HEAD_DIM // 4
D_RNN = D_MODEL
RNN_BLOCK = 128
N_RNN_BLOCKS = D_RNN // RNN_BLOCK
CONV_W = 4
CONV_PAD_L = 2
C_DECAY = 8.0
D_FF = 5632
NORM_EPS = 1e-6

V7X_VMEM_USABLE_BYTES = 58 * 1024 * 1024
SUBLANES = 8
LANES = 128

F32 = jnp.float32
BF16 = jnp.bfloat16
LOG2_E = math.log2(math.e)


def _nbytes(shape, dtype):
    return int(np.prod(shape)) * jnp.dtype(dtype).itemsize


def _compiler_params(semantics, vmem_estimate_bytes):
    limit = min(V7X_VMEM_USABLE_BYTES, max(32 * 1024 * 1024, int(vmem_estimate_bytes)))
    return pltpu.CompilerParams(dimension_semantics=semantics, vmem_limit_bytes=limit)


def _rms_scale(x):
    return lax.rsqrt(jnp.mean(x * x, axis=-1, keepdims=True) + NORM_EPS)


def _stacked_spec(block_shape, lead, index_map):
    return pl.BlockSpec((None,) * len(lead) + tuple(block_shape),
                        lambda *g: tuple(lead) + tuple(index_map(*g)))


def _rms_norm_kernel(x_ref, g_ref, o_ref):
    x = x_ref[...]
    o_ref[...] = (x * _rms_scale(x) * g_ref[...]).astype(o_ref.dtype)


def _rms_norm(x, gain, out_dtype, block_rows=512):
    s, d = x.shape
    est = 2 * (_nbytes((block_rows, d), F32) + _nbytes((block_rows, d), out_dtype)) \
        + 2 * _nbytes((block_rows, d), F32)
    return pl.pallas_call(
        _rms_norm_kernel,
        grid=(s // block_rows,),
        in_specs=[pl.BlockSpec((block_rows, d), lambda i: (i, 0)),
                  pl.BlockSpec((1, d), lambda i: (0, 0))],
        out_specs=pl.BlockSpec((block_rows, d), lambda i: (i, 0)),
        out_shape=jax.ShapeDtypeStruct((s, d), out_dtype),
        name="rms_norm",
        compiler_params=_compiler_params(("parallel",), est),
    )(x, gain.reshape(1, d))


def _swiglu_epilogue(a, b):
    return ((a / (1.0 + jnp.exp(-a))) * b,)


def _gelu_split_epilogue(a, b):
    c = math.sqrt(2.0 / math.pi)
    cdf = 0.5 * (1.0 + jnp.tanh(c * (a + 0.044715 * (a * a * a))))
    return (a * cdf, b)


def _dual_matmul_kernel(x_ref, wa_ref, wb_ref, *refs, epilogue, has_side):
    if has_side:
        side_ref, *out_refs, side_out_ref = refs
        side_out_ref[...] = side_ref[...].astype(side_out_ref.dtype)
    else:
        out_refs = refs
    x = x_ref[...]
    a = jnp.dot(x, wa_ref[...].astype(BF16), preferred_element_type=F32)
    b = jnp.dot(x, wb_ref[...].astype(BF16), preferred_element_type=F32)
    for ref, val in zip(out_refs, epilogue(a, b)):
        ref[...] = val.astype(ref.dtype)


def _dual_matmul(x, w, lead, epilogue, out_dtypes, name, side=None, block_rows=1024,
                 block_cols=512):
    s, k = x.shape
    n = w.shape[-1] // 2
    n_col_blocks = n // block_cols
    n_row_blocks = s // block_rows
    est = 2 * _nbytes((block_rows, k), BF16) + 4 * _nbytes((k, block_cols), w.dtype) \
        + 2 * _nbytes((k, block_cols), BF16) \
        + sum(2 * _nbytes((block_rows, block_cols), dt) for dt in out_dtypes) \
        + 4 * _nbytes((block_rows, block_cols), F32)
    out_spec = pl.BlockSpec((block_rows, block_cols), lambda i, j: (i, j))
    in_specs = [pl.BlockSpec((block_rows, k), lambda i, j: (i, 0)),
                _stacked_spec((k, block_cols), lead, lambda i, j: (0, j)),
                _stacked_spec((k, block_cols), lead, lambda i, j: (0, j + n_col_blocks))]
    out_specs = [out_spec] * len(out_dtypes)
    out_shape = [jax.ShapeDtypeStruct((s, n), dt) for dt in out_dtypes]
    operands = [x, w, w]
    if side is not None:
        side_w, side_lead = side
        side_rows, side_cols = side_w.shape[-2:]
        slab, rem = divmod(side_rows, n_row_blocks * n_col_blocks)
        assert rem == 0 and slab % (2 * SUBLANES) == 0
        slab_index = lambda i, j: (i * n_col_blocks + j, 0)
        in_specs.append(_stacked_spec((slab, side_cols), side_lead, slab_index))
        out_specs.append(pl.BlockSpec((slab, side_cols), slab_index))
        out_shape.append(jax.ShapeDtypeStruct((side_rows, side_cols), BF16))
        operands.append(side_w)
        est += 2 * _nbytes((slab, side_cols), F32) + 2 * _nbytes((slab, side_cols), BF16)
    return pl.pallas_call(
        functools.partial(_dual_matmul_kernel, epilogue=epilogue, has_side=side is not None),
        grid=(n_row_blocks, n_col_blocks),
        in_specs=in_specs,
        out_specs=out_specs,
        out_shape=out_shape,
        name=name,
        compiler_params=_compiler_params(("parallel", "arbitrary"), est),
    )(*operands)


def _proj_residual_norm_kernel(y_ref, w_ref, x_ref, g_ref, xo_ref, xn_ref, *, branch_scale):
    acc = jnp.dot(y_ref[...], w_ref[...], preferred_element_type=F32)
    x_new = x_ref[...] + branch_scale * acc
    xo_ref[...] = x_new
    xn_ref[...] = (x_new * _rms_scale(x_new) * g_ref[...]).astype(xn_ref.dtype)


def _proj_residual_norm(y, w, lead, x, next_gain, branch_scale, norm_dtype, block_rows):
    s, k = y.shape
    d = w.shape[-1]
    est = _nbytes((k, d), BF16) + 2 * _nbytes((block_rows, k), BF16) \
        + 4 * _nbytes((block_rows, d), F32) + 2 * _nbytes((block_rows, d), norm_dtype) \
        + 3 * _nbytes((block_rows, d), F32)
    row_spec = lambda cols: pl.BlockSpec((block_rows, cols), lambda i: (i, 0))
    w_spec = pl.BlockSpec((None,) * len(lead) + (k, d), lambda i: tuple(lead) + (0, 0),
                          pipeline_mode=pl.Buffered(1))
    return pl.pallas_call(
        functools.partial(_proj_residual_norm_kernel, branch_scale=branch_scale),
        grid=(s // block_rows,),
        in_specs=[row_spec(k), w_spec, row_spec(d), pl.BlockSpec((1, d), lambda i: (0, 0))],
        out_specs=[row_spec(d), row_spec(d)],
        out_shape=[jax.ShapeDtypeStruct((s, d), F32), jax.ShapeDtypeStruct((s, d), norm_dtype)],
        name=f"proj_residual_norm_k{k}",
        compiler_params=_compiler_params(("parallel",), est),
    )(y, w, x, next_gain.reshape(1, d))


KV_CHUNK = 512
QKV_COLS_PER_DOT = 512


def _swap_rotary_halves(x):
    lane = lax.broadcasted_iota(jnp.int32, x.shape, 1)
    first_quarter = (lane % (2 * ROPE_FREQS)) < ROPE_FREQS
    return jnp.where(first_quarter,
                     pltpu.roll(x, HEAD_DIM - ROPE_FREQS, 1),
                     pltpu.roll(x, ROPE_FREQS, 1))


def _qkv_kernel(x_ref, w_ref, gain_ref, cos_ref, sin_ref, q_ref, kt_ref, v_ref):
    x = x_ref[...]
    cos_t = cos_ref[...]
    sin_t = sin_ref[...]
    heads_per_dot = QKV_COLS_PER_DOT // HEAD_DIM

    def normed_rotated(acc, h, col0):
        cols = slice(col0 + h * HEAD_DIM, col0 + (h + 1) * HEAD_DIM)
        xh = acc[:, h * HEAD_DIM:(h + 1) * HEAD_DIM]
        xh = xh * _rms_scale(xh) * gain_ref[:, cols]
        return xh * cos_t + _swap_rotary_halves(xh) * sin_t

    for jb in range(QKV_DIM // QKV_COLS_PER_DOT):
        col0 = jb * QKV_COLS_PER_DOT
        acc = jnp.dot(x, w_ref[:, col0:col0 + QKV_COLS_PER_DOT], preferred_element_type=F32)
        if col0 < Q_DIM:
            for h in range(heads_per_dot):
                q_ref[:, col0 + h * HEAD_DIM:col0 + (h + 1) * HEAD_DIM] = \
                    normed_rotated(acc, h, col0).astype(q_ref.dtype)
        elif col0 < Q_DIM + KV_DIM:
            for h in range(heads_per_dot):
                row0 = col0 - Q_DIM + h * HEAD_DIM
                kt_ref[row0:row0 + HEAD_DIM, :] = \
                    normed_rotated(acc, h, col0).T.astype(kt_ref.dtype)
        else:
            v0 = col0 - Q_DIM - KV_DIM
            v_ref[:, v0:v0 + QKV_COLS_PER_DOT] = acc.astype(v_ref.dtype)


def _qkv_proj(xn, w_qkv, lead, head_gain_cols, cos_t, sin_t):
    s, k = xn.shape
    rows = KV_CHUNK
    est = _nbytes((k, QKV_DIM), BF16) + 2 * _nbytes((rows, k), BF16) \
        + 4 * _nbytes((rows, HEAD_DIM), F32) + 2 * _nbytes((rows, QKV_DIM), BF16) \
        + 6 * _nbytes((rows, QKV_COLS_PER_DOT), F32)
    return pl.pallas_call(
        _qkv_kernel,
        grid=(s // rows,),
        in_specs=[pl.BlockSpec((rows, k), lambda i: (i, 0)),
                  pl.BlockSpec((None,) * len(lead) + (k, QKV_DIM),
                               lambda i: tuple(lead) + (0, 0), pipeline_mode=pl.Buffered(1)),
                  pl.BlockSpec((1, QKV_DIM), lambda i: (0, 0)),
                  pl.BlockSpec((rows, HEAD_DIM), lambda i: (i, 0)),
                  pl.BlockSpec((rows, HEAD_DIM), lambda i: (i, 0))],
        out_specs=[pl.BlockSpec((rows, Q_DIM), lambda i: (i, 0)),
                   pl.BlockSpec((None, KV_DIM, rows), lambda i: (i, 0, 0)),
                   pl.BlockSpec((rows, KV_DIM), lambda i: (i, 0))],
        out_shape=[jax.ShapeDtypeStruct((s, Q_DIM), BF16),
                   jax.ShapeDtypeStruct((s // rows, KV_DIM, rows), BF16),
                   jax.ShapeDtypeStruct((s, KV_DIM), BF16)],
        name="qkv_proj",
        compiler_params=_compiler_params(("parallel",), est),
    )(xn, w_qkv, head_gain_cols, cos_t, sin_t)


def _rotary_tables(s):
    rows = jnp.repeat(jnp.arange(s // GRID_W, dtype=F32), GRID_W)
    cols = jnp.tile(jnp.arange(GRID_W, dtype=F32), s // GRID_W)
    inv_freq = ROPE_THETA ** (-jnp.arange(ROPE_FREQS, dtype=F32) / ROPE_FREQS)
    ang_row = rows[:, None] * inv_freq[None, :]
    ang_col = cols[:, None] * inv_freq[None, :]
    cos_t = jnp.concatenate([jnp.cos(ang_row)] * 2 + [jnp.cos(ang_col)] * 2, axis=-1)
    sin_t = jnp.concatenate([-jnp.sin(ang_row), jnp.sin(ang_row),
                             -jnp.sin(ang_col), jnp.sin(ang_col)], axis=-1)
    return cos_t, sin_t


FLASH_CHUNKS_PER_STEP = 8


def _flash_kernel(q_ref, kt_ref, v_ref, o_ref, q_all, s_ref, m_ref, l_ref, acc_ref):
    tq = q_ref.shape[0]
    n_chunks = kt_ref.shape[0]
    lane_tiles = KV_CHUNK // LANES

    for g in range(GROUP):
        q_all[g * tq:(g + 1) * tq, :] = q_ref[:, g * HEAD_DIM:(g + 1) * HEAD_DIM]

    def scores(c):
        return jnp.dot(q_all[...], kt_ref[c], preferred_element_type=F32)

    def absorb(c, s, m, l, acc):
        tiles = [s[:, t * LANES:(t + 1) * LANES] for t in range(lane_tiles)]
        tile_max = functools.reduce(jnp.maximum, tiles)
        m_new = jnp.maximum(m, jnp.max(tile_max, axis=-1, keepdims=True))
        alpha = jnp.exp2(m - m_new)
        p_tiles = [jnp.exp2(t - m_new) for t in tiles]
        l = alpha * l + functools.reduce(jnp.add, p_tiles)
        p = jnp.concatenate(p_tiles, axis=1).astype(BF16)
        start = pl.multiple_of(c * KV_CHUNK, KV_CHUNK)
        pv = jnp.dot(p, v_ref[pl.ds(start, KV_CHUNK), :], preferred_element_type=F32)
        return m_new, l, alpha * acc + pv

    s_ref[...] = scores(0)
    m_ref[...] = jnp.full(m_ref.shape, -jnp.inf, F32)
    l_ref[...] = jnp.zeros(l_ref.shape, F32)
    acc_ref[...] = jnp.zeros(acc_ref.shape, F32)

    def chunk_group(c0, is_last):
        state = (m_ref[...], l_ref[...], acc_ref[...])
        s_cur = s_ref[...]
        for u in range(FLASH_CHUNKS_PER_STEP):
            final_chunk = is_last and u == FLASH_CHUNKS_PER_STEP - 1
            s_next = None if final_chunk else scores(c0 + u + 1)
            state = absorb(c0 + u, s_cur, *state)
            s_cur = s_next
        m_ref[...], l_ref[...], acc_ref[...] = state
        if not is_last:
            s_ref[...] = s_cur

    n_groups = n_chunks // FLASH_CHUNKS_PER_STEP

    def group_body(it, _):
        chunk_group(it * FLASH_CHUNKS_PER_STEP, False)
        return 0

    lax.fori_loop(0, n_groups - 1, group_body, 0)
    chunk_group((n_groups - 1) * FLASH_CHUNKS_PER_STEP, True)

    for g in range(GROUP):
        r = slice(g * tq, (g + 1) * tq)
        l_row = jnp.sum(l_ref[r, :], axis=-1, keepdims=True)
        o_ref[:, g * HEAD_DIM:(g + 1) * HEAD_DIM] = (acc_ref[r, :] / l_row).astype(o_ref.dtype)


def _flash_attention(q, kt, v, block_q=256):
    s = q.shape[0]
    n_chunks = s // KV_CHUNK
    assert n_chunks % FLASH_CHUNKS_PER_STEP == 0
    group_cols = GROUP * HEAD_DIM
    rows = GROUP * block_q
    stat = pltpu.VMEM((rows, LANES), F32)
    est = 4 * _nbytes((block_q, group_cols), BF16) + 4 * _nbytes((s, HEAD_DIM), BF16) \
        + _nbytes((rows, HEAD_DIM), BF16) + _nbytes((rows, KV_CHUNK), F32) \
        + 3 * _nbytes((rows, LANES), F32) + 4 * _nbytes((rows, KV_CHUNK), F32)
    return pl.pallas_call(
        _flash_kernel,
        grid=(N_KV_HEADS, s // block_q),
        in_specs=[pl.BlockSpec((block_q, group_cols), lambda h, i: (i, h)),
                  pl.BlockSpec((n_chunks, HEAD_DIM, KV_CHUNK), lambda h, i: (0, h, 0)),
                  pl.BlockSpec((s, HEAD_DIM), lambda h, i: (0, h))],
        out_specs=pl.BlockSpec((block_q, group_cols), lambda h, i: (i, h)),
        out_shape=jax.ShapeDtypeStruct((s, Q_DIM), BF16),
        scratch_shapes=[pltpu.VMEM((rows, HEAD_DIM), BF16),
                        pltpu.VMEM((rows, KV_CHUNK), F32), stat, stat, stat],
        name="flash_attention",
        compiler_params=_compiler_params(("parallel", "parallel"), est),
    )(q, kt, v)


RGLRU_CHUNK = 512
SCAN_UNROLL = 8
SCAN_BLOCK = SCAN_UNROLL * SUBLANES
CONV_HALO = SUBLANES


def _scan8(a, b, reverse):
    row = lax.broadcasted_iota(jnp.int32, a.shape, 0)
    for d in (1, 2, 4):
        shift = (SUBLANES - d) if reverse else d
        valid = (row < SUBLANES - d) if reverse else (row >= d)
        a_prev = jnp.where(valid, pltpu.roll(a, shift, 0), 1.0)
        b_prev = jnp.where(valid, pltpu.roll(b, shift, 0), 0.0)
        b = a * b_prev + b
        a = a * a_prev
    return a, b


def _rglru_kernel(rec_ref, gate_ref, cw_ref, cb_ref, gw_ref, gb_ref, lam_ref, y_ref,
                  xpad_ref, af_ref, bf_ref, ab_ref, bb_ref, hb_ref):
    s_len = rec_ref.shape[0]
    n_chunks = s_len // RGLRU_CHUNK

    zeros_halo = jnp.zeros((CONV_HALO, LANES), F32)
    xpad_ref[0:CONV_HALO, :] = zeros_halo
    xpad_ref[CONV_HALO + s_len:CONV_HALO + s_len + CONV_HALO, :] = zeros_halo
    xpad_ref[CONV_HALO:CONV_HALO + s_len, :] = rec_ref[...]

    neg_lam = -lam_ref[...]
    softplus = jnp.maximum(neg_lam, 0.0) + jnp.log(1.0 + jnp.exp(-jnp.abs(neg_lam)))
    half_decay = (-0.5 * C_DECAY * LOG2_E) * softplus

    def gates_body(c, _):
        start = pl.multiple_of(c * RGLRU_CHUNK, RGLRU_CHUNK)
        xc = cb_ref[...]
        for tap in range(CONV_W):
            off = CONV_HALO - CONV_PAD_L + tap
            xc = xc + xpad_ref[pl.ds(start + off, RGLRU_CHUNK), :] * cw_ref[tap:tap + 1, :]
        t = jnp.tanh(jnp.dot(xc.astype(BF16), gw_ref[...], preferred_element_type=F32)
                     + gb_ref[...])
        xc_half = 0.5 * xc
        for direction, (a_ref, b_ref) in enumerate(((af_ref, bf_ref), (ab_ref, bb_ref))):
            t_r = t[:, (2 * direction) * LANES:(2 * direction + 1) * LANES]
            t_i = t[:, (2 * direction + 1) * LANES:(2 * direction + 2) * LANES]
            hd = half_decay[direction:direction + 1, :]
            a = jnp.exp2(hd * t_r + hd)
            a_ref[pl.ds(start, RGLRU_CHUNK), :] = a
            b_ref[pl.ds(start, RGLRU_CHUNK), :] = \
                jnp.sqrt(1.0 - a * a) * (xc_half + xc_half * t_i)
        return 0

    lax.fori_loop(0, n_chunks, gates_body, 0)

    n_iters = s_len // SCAN_BLOCK

    def scan_block(a_ref, b_ref, h_ref, h_row0, start, carry, reverse):
        a_blk = a_ref[pl.ds(start, SCAN_BLOCK), :]
        b_blk = b_ref[pl.ds(start, SCAN_BLOCK), :]
        order = range(SCAN_UNROLL - 1, -1, -1) if reverse else range(SCAN_UNROLL)
        last = 0 if reverse else SUBLANES - 1
        h_groups = [None] * SCAN_UNROLL
        for u in order:
            rows = slice(u * SUBLANES, (u + 1) * SUBLANES)
            a8, b8 = _scan8(a_blk[rows], b_blk[rows], reverse)
            a_out = jnp.broadcast_to(a8[last:last + 1, :], a8.shape)
            b_out = jnp.broadcast_to(b8[last:last + 1, :], b8.shape)
            h_groups[u] = a8 * carry + b8
            carry = a_out * carry + b_out
        h_ref[pl.ds(h_row0 + start, SCAN_BLOCK), :] = jnp.concatenate(h_groups, axis=0)
        return carry

    def scan_body(it, carry):
        h_f, h_b = carry
        start_f = pl.multiple_of(it * SCAN_BLOCK, SCAN_BLOCK)
        start_b = pl.multiple_of((n_iters - 1 - it) * SCAN_BLOCK, SCAN_BLOCK)
        h_f = scan_block(af_ref, bf_ref, xpad_ref, CONV_HALO, start_f, h_f, reverse=False)
        h_b = scan_block(ab_ref, bb_ref, hb_ref, 0, start_b, h_b, reverse=True)
        return h_f, h_b

    zero_state = jnp.zeros((SUBLANES, LANES), F32)
    lax.fori_loop(0, n_iters, scan_body, (zero_state, zero_state))

    def out_body(c, _):
        start = pl.multiple_of(c * RGLRU_CHUNK, RGLRU_CHUNK)
        rows = pl.ds(start, RGLRU_CHUNK)
        h = xpad_ref[pl.ds(CONV_HALO + start, RGLRU_CHUNK), :] + hb_ref[rows, :]
        y_ref[rows, :] = (gate_ref[rows, :] * h).astype(y_ref.dtype)
        return 0

    lax.fori_loop(0, n_chunks, out_body, 0)


def _rglru(rec, gelu_gate, conv_w, conv_b, gate_w_half, gate_b_half, lam):
    s = rec.shape[0]
    col_spec = lambda rows: pl.BlockSpec((rows, LANES), lambda j: (0, j))
    seq_f32 = _nbytes((s, LANES), F32)
    est = 4 * seq_f32 + 2 * _nbytes((s, LANES), BF16) + 6 * seq_f32 \
        + 12 * _nbytes((RGLRU_CHUNK, 4 * LANES), F32)
    return pl.pallas_call(
        _rglru_kernel,
        grid=(N_RNN_BLOCKS,),
        in_specs=[col_spec(s), col_spec(s), col_spec(CONV_W), col_spec(1),
                  pl.BlockSpec((None, RNN_BLOCK, 4 * RNN_BLOCK), lambda j: (j, 0, 0)),
                  pl.BlockSpec((None, 1, 4 * RNN_BLOCK), lambda j: (j, 0, 0)),
                  col_spec(2)],
        out_specs=col_spec(s),
        out_shape=jax.ShapeDtypeStruct((s, D_RNN), BF16),
        scratch_shapes=[pltpu.VMEM((s + 2 * CONV_HALO, LANES), F32)]
        + [pltpu.VMEM((s, LANES), F32)] * 5,
        name="rglru",
        compiler_params=_compiler_params(("parallel",), est),
    )(rec, gelu_gate, conv_w, conv_b.reshape(1, D_RNN), gate_w_half, gate_b_half, lam)


def kernel(x, ffn_norm, ffn_w_gu, ffn_w_down, attn_norm, attn_w_qkv, attn_q_norm, attn_k_norm,
           attn_w_o, rec_norm, rec_w_in, rec_conv_w, rec_conv_b, rec_gate_w, rec_gate_b,
           rec_lambda, rec_w_out, final_norm):
    b, s, d = x.shape
    assert (b, d) == (1, D_MODEL) and s % 1024 == 0
    x = x.reshape(s, d)
    cos_t, sin_t = _rotary_tables(s)
    w_qkv = attn_w_qkv.astype(BF16)
    w_o = attn_w_o.astype(BF16)
    w_out = rec_w_out.astype(BF16)

    def ffn(x, xn, layer, half, next_gain, norm_dtype):
        h, w_down = _dual_matmul(xn, ffn_w_gu, (layer, half), _swiglu_epilogue, (BF16,),
                                 "ffn_gate_up", side=(ffn_w_down, (layer, half)))
        return _proj_residual_norm(h, w_down, (), x, next_gain, 0.5, norm_dtype, block_rows=256)

    xn = _rms_norm(x, ffn_norm[0, 0], BF16)
    for i in range(DEPTH):
        j = i // N_MIXERS
        is_attn = i % N_MIXERS == 0
        x, xn = ffn(x, xn, i, 0, attn_norm[j] if is_attn else rec_norm[j], BF16)
        if is_attn:
            q_scale = (HEAD_DIM ** -0.5) * LOG2_E
            head_gain_cols = jnp.concatenate(
                [jnp.tile(attn_q_norm[j] * q_scale, N_HEADS),
                 jnp.tile(attn_k_norm[j], N_KV_HEADS),
                 jnp.ones((KV_DIM,), F32)]).reshape(1, QKV_DIM)
            q, kt, v = _qkv_proj(xn, w_qkv, (j,), head_gain_cols, cos_t, sin_t)
            y = _flash_attention(q, kt, v)
            w_mix, lead = w_o, (j,)
        else:
            gelu_gate, rec = _dual_matmul(xn, rec_w_in, (j,), _gelu_split_epilogue, (F32, F32),
                                          "rec_in_proj")
            gate_w_half = (0.5 * jnp.transpose(rec_gate_w[j], (2, 3, 0, 1, 4))).reshape(
                N_RNN_BLOCKS, RNN_BLOCK, 4 * RNN_BLOCK).astype(BF16)
            gate_b_half = 0.5 * jnp.transpose(
                rec_gate_b[j].reshape(2, 2, N_RNN_BLOCKS, RNN_BLOCK), (2, 0, 1, 3)).reshape(
                N_RNN_BLOCKS, 1, 4 * RNN_BLOCK)
            y = _rglru(rec, gelu_gate, rec_conv_w[j], rec_conv_b[j], gate_w_half, gate_b_half,
                       rec_lambda[j])
            w_mix, lead = w_out, (j,)
        x, xn = _proj_residual_norm(y, w_mix, lead, x, ffn_norm[i, 1], 1.0, BF16, block_rows=512)
        last = i == DEPTH - 1
        next_gain = final_norm if last else ffn_norm[i + 1, 0]
        x, xn = ffn(x, xn, i, 1, next_gain, F32 if last else BF16)
    return xn.reshape(b, s, d)
```

```python
import functools
import math

import jax
import jax.numpy as jnp
import numpy as np
from jax import lax
from jax.experimental import pallas as pl
from jax.experimental.pallas import tpu as pltpu

D_MODEL = 2048
DEPTH = 4
N_MIXERS = 2
GRID_W = 64
HEAD_DIM = 128
N_HEADS = D_MODEL // HEAD_DIM
N_KV_HEADS = 4
GROUP = N_HEADS // N_KV_HEADS
Q_DIM = N_HEADS * HEAD_DIM
KV_DIM = N_KV_HEADS * HEAD_DIM
QKV_DIM = Q_DIM + 2 * KV_DIM
ROPE_THETA = 10000.0
ROPE_FREQS = HEAD_DIM // 4
D_RNN = D_MODEL
RNN_BLOCK = 128
N_RNN_BLOCKS = D_RNN // RNN_BLOCK
CONV_W = 4
CONV_PAD_L = 2
C_DECAY = 8.0
D_FF = 5632
NORM_EPS = 1e-6

V7X_VMEM_USABLE_BYTES = 58 * 1024 * 1024
SUBLANES = 8
LANES = 128

F32 = jnp.float32
BF16 = jnp.bfloat16
LOG2_E = math.log2(math.e)


def _nbytes(shape, dtype):
    return int(np.prod(shape)) * jnp.dtype(dtype).itemsize


def _compiler_params(semantics, vmem_estimate_bytes):
    limit = min(V7X_VMEM_USABLE_BYTES, max(32 * 1024 * 1024, int(vmem_estimate_bytes)))
    return pltpu.CompilerParams(dimension_semantics=semantics, vmem_limit_bytes=limit)


def _rms_scale(x):
    return lax.rsqrt(jnp.mean(x * x, axis=-1, keepdims=True) + NORM_EPS)


def _stacked_spec(block_shape, lead, index_map):
    return pl.BlockSpec((None,) * len(lead) + tuple(block_shape),
                        lambda *g: tuple(lead) + tuple(index_map(*g)))


def _rms_norm_kernel(x_ref, g_ref, o_ref):
    x = x_ref[...]
    o_ref[...] = (x * _rms_scale(x) * g_ref[...]).astype(o_ref.dtype)


def _rms_norm(x, gain, out_dtype, block_rows=512):
    s, d = x.shape
    est = 2 * (_nbytes((block_rows, d), F32) + _nbytes((block_rows, d), out_dtype)) \
        + 2 * _nbytes((block_rows, d), F32)
    return pl.pallas_call(
        _rms_norm_kernel,
        grid=(s // block_rows,),
        in_specs=[pl.BlockSpec((block_rows, d), lambda i: (i, 0)),
                  pl.BlockSpec((1, d), lambda i: (0, 0))],
        out_specs=pl.BlockSpec((block_rows, d), lambda i: (i, 0)),
        out_shape=jax.ShapeDtypeStruct((s, d), out_dtype),
        name="rms_norm",
        compiler_params=_compiler_params(("parallel",), est),
    )(x, gain.reshape(1, d))


def _swiglu_epilogue(a, b):
    return ((a / (1.0 + jnp.exp(-a))) * b,)


def _gelu_split_epilogue(a, b):
    c = math.sqrt(2.0 / math.pi)
    cdf = 0.5 * (1.0 + jnp.tanh(c * (a + 0.044715 * (a * a * a))))
    return (a * cdf, b)


def _dual_matmul_kernel(x_ref, wa_ref, wb_ref, *refs, epilogue, has_side):
    if has_side:
        side_ref, *out_refs, side_out_ref = refs
        side_out_ref[...] = side_ref[...].astype(side_out_ref.dtype)
    else:
        out_refs = refs
    x = x_ref[...]
    a = jnp.dot(x, wa_ref[...].astype(BF16), preferred_element_type=F32)
    b = jnp.dot(x, wb_ref[...].astype(BF16), preferred_element_type=F32)
    for ref, val in zip(out_refs, epilogue(a, b)):
        ref[...] = val.astype(ref.dtype)


def _dual_matmul(x, w, lead, epilogue, out_dtypes, name, side=None, block_rows=1024,
                 block_cols=512):
    s, k = x.shape
    n = w.shape[-1] // 2
    n_col_blocks = n // block_cols
    n_row_blocks = s // block_rows
    est = 2 * _nbytes((block_rows, k), BF16) + 4 * _nbytes((k, block_cols), w.dtype) \
        + 2 * _nbytes((k, block_cols), BF16) \
        + sum(2 * _nbytes((block_rows, block_cols), dt) for dt in out_dtypes) \
        + 4 * _nbytes((block_rows, block_cols), F32)
    out_spec = pl.BlockSpec((block_rows, block_cols), lambda i, j: (i, j))
    in_specs = [pl.BlockSpec((block_rows, k), lambda i, j: (i, 0)),
                _stacked_spec((k, block_cols), lead, lambda i, j: (0, j)),
                _stacked_spec((k, block_cols), lead, lambda i, j: (0, j + n_col_blocks))]
    out_specs = [out_spec] * len(out_dtypes)
    out_shape = [jax.ShapeDtypeStruct((s, n), dt) for dt in out_dtypes]
    operands = [x, w, w]
    if side is not None:
        side_w, side_lead = side
        side_rows, side_cols = side_w.shape[-2:]
        slab, rem = divmod(side_rows, n_row_blocks * n_col_blocks)
        assert rem == 0 and slab % (2 * SUBLANES) == 0
        slab_index = lambda i, j: (i * n_col_blocks + j, 0)
        in_specs.append(_stacked_spec((slab, side_cols), side_lead, slab_index))
        out_specs.append(pl.BlockSpec((slab, side_cols), slab_index))
        out_shape.append(jax.ShapeDtypeStruct((side_rows, side_cols), BF16))
        operands.append(side_w)
        est += 2 * _nbytes((slab, side_cols), F32) + 2 * _nbytes((slab, side_cols), BF16)
    return pl.pallas_call(
        functools.partial(_dual_matmul_kernel, epilogue=epilogue, has_side=side is not None),
        grid=(n_row_blocks, n_col_blocks),
        in_specs=in_specs,
        out_specs=out_specs,
        out_shape=out_shape,
        name=name,
        compiler_params=_compiler_params(("parallel", "arbitrary"), est),
    )(*operands)


def _proj_residual_norm_kernel(y_ref, w_ref, x_ref, g_ref, xo_ref, xn_ref, *, branch_scale):
    acc = jnp.dot(y_ref[...], w_ref[...], preferred_element_type=F32)
    x_new = x_ref[...] + branch_scale * acc
    xo_ref[...] = x_new
    xn_ref[...] = (x_new * _rms_scale(x_new) * g_ref[...]).astype(xn_ref.dtype)


def _proj_residual_norm(y, w, lead, x, next_gain, branch_scale, norm_dtype, block_rows):
    s, k = y.shape
    d = w.shape[-1]
    est = _nbytes((k, d), BF16) + 2 * _nbytes((block_rows, k), BF16) \
        + 4 * _nbytes((block_rows, d), F32) + 2 * _nbytes((block_rows, d), norm_dtype) \
        + 3 * _nbytes((block_rows, d), F32)
    row_spec = lambda cols: pl.BlockSpec((block_rows, cols), lambda i: (i, 0))
    w_spec = pl.BlockSpec((None,) * len(lead) + (k, d), lambda i: tuple(lead) + (0, 0),
                          pipeline_mode=pl.Buffered(1))
    return pl.pallas_call(
        functools.partial(_proj_residual_norm_kernel, branch_scale=branch_scale),
        grid=(s // block_rows,),
        in_specs=[row_spec(k), w_spec, row_spec(d), pl.BlockSpec((1, d), lambda i: (0, 0))],
        out_specs=[row_spec(d), row_spec(d)],
        out_shape=[jax.ShapeDtypeStruct((s, d), F32), jax.ShapeDtypeStruct((s, d), norm_dtype)],
        name=f"proj_residual_norm_k{k}",
        compiler_params=_compiler_params(("parallel",), est),
    )(y, w, x, next_gain.reshape(1, d))


KV_CHUNK = 512
QKV_COLS_PER_DOT = 512


def _swap_rotary_halves(x):
    lane = lax.broadcasted_iota(jnp.int32, x.shape, 1)
    first_quarter = (lane % (2 * ROPE_FREQS)) < ROPE_FREQS
    return jnp.where(first_quarter,
                     pltpu.roll(x, HEAD_DIM - ROPE_FREQS, 1),
                     pltpu.roll(x, ROPE_FREQS, 1))


def _qkv_kernel(x_ref, w_ref, gain_ref, cos_ref, sin_ref, q_ref, kt_ref, v_ref):
    x = x_ref[...]
    cos_t = cos_ref[...]
    sin_t = sin_ref[...]
    heads_per_dot = QKV_COLS_PER_DOT // HEAD_DIM

    def normed_rotated(acc, h, col0):
        cols = slice(col0 + h * HEAD_DIM, col0 + (h + 1) * HEAD_DIM)
        xh = acc[:, h * HEAD_DIM:(h + 1) * HEAD_DIM]
        xh = xh * _rms_scale(xh) * gain_ref[:, cols]
        return xh * cos_t + _swap_rotary_halves(xh) * sin_t

    for jb in range(QKV_DIM // QKV_COLS_PER_DOT):
        col0 = jb * QKV_COLS_PER_DOT
        acc = jnp.dot(x, w_ref[:, col0:col0 + QKV_COLS_PER_DOT], preferred_element_type=F32)
        if col0 < Q_DIM:
            for h in range(heads_per_dot):
                q_ref[:, col0 + h * HEAD_DIM:col0 + (h + 1) * HEAD_DIM] = \
                    normed_rotated(acc, h, col0).astype(q_ref.dtype)
        elif col0 < Q_DIM + KV_DIM:
            for h in range(heads_per_dot):
                row0 = col0 - Q_DIM + h * HEAD_DIM
                kt_ref[row0:row0 + HEAD_DIM, :] = \
                    normed_rotated(acc, h, col0).T.astype(kt_ref.dtype)
        else:
            v0 = col0 - Q_DIM - KV_DIM
            v_ref[:, v0:v0 + QKV_COLS_PER_DOT] = acc.astype(v_ref.dtype)


def _qkv_proj(xn, w_qkv, lead, head_gain_cols, cos_t, sin_t):
    s, k = xn.shape
    rows = KV_CHUNK
    est = _nbytes((k, QKV_DIM), BF16) + 2 * _nbytes((rows, k), BF16) \
        + 4 * _nbytes((rows, HEAD_DIM), F32) + 2 * _nbytes((rows, QKV_DIM), BF16) \
        + 6 * _nbytes((rows, QKV_COLS_PER_DOT), F32)
    return pl.pallas_call(
        _qkv_kernel,
        grid=(s // rows,),
        in_specs=[pl.BlockSpec((rows, k), lambda i: (i, 0)),
                  pl.BlockSpec((None,) * len(lead) + (k, QKV_DIM),
                               lambda i: tuple(lead) + (0, 0), pipeline_mode=pl.Buffered(1)),
                  pl.BlockSpec((1, QKV_DIM), lambda i: (0, 0)),
                  pl.BlockSpec((rows, HEAD_DIM), lambda i: (i, 0)),
                  pl.BlockSpec((rows, HEAD_DIM), lambda i: (i, 0))],
        out_specs=[pl.BlockSpec((rows, Q_DIM), lambda i: (i, 0)),
                   pl.BlockSpec((None, KV_DIM, rows), lambda i: (i, 0, 0)),
                   pl.BlockSpec((rows, KV_DIM), lambda i: (i, 0))],
        out_shape=[jax.ShapeDtypeStruct((s, Q_DIM), BF16),
                   jax.ShapeDtypeStruct((s // rows, KV_DIM, rows), BF16),
                   jax.ShapeDtypeStruct((s, KV_DIM), BF16)],
        name="qkv_proj",
        compiler_params=_compiler_params(("parallel",), est),
    )(xn, w_qkv, head_gain_cols, cos_t, sin_t)


def _rotary_tables(s):
    n_rows = s // GRID_W
    inv_freq = ROPE_THETA ** (-jnp.arange(ROPE_FREQS, dtype=F32) / ROPE_FREQS)
    ang_row = jnp.arange(n_rows, dtype=F32)[:, None] * inv_freq[None, :]
    ang_col = jnp.arange(GRID_W, dtype=F32)[:, None] * inv_freq[None, :]
    by_row = lambda t: jnp.repeat(t, GRID_W, axis=0)
    by_col = lambda t: jnp.tile(t, (n_rows, 1))
    cos_r, sin_r = by_row(jnp.cos(ang_row)), by_row(jnp.sin(ang_row))
    cos_c, sin_c = by_col(jnp.cos(ang_col)), by_col(jnp.sin(ang_col))
    cos_t = jnp.concatenate([cos_r, cos_r, cos_c, cos_c], axis=-1)
    sin_t = jnp.concatenate([-sin_r, sin_r, -sin_c, sin_c], axis=-1)
    return cos_t, sin_t


FLASH_CHUNKS_PER_STEP = 4


def _flash_kernel(q_ref, kt_ref, v_ref, o_ref, q_all, s_ref, m_ref, l_ref, acc_ref):
    tq = q_ref.shape[0]
    n_chunks = kt_ref.shape[0]
    lane_tiles = KV_CHUNK // LANES

    for g in range(GROUP):
        q_all[g * tq:(g + 1) * tq, :] = q_ref[:, g * HEAD_DIM:(g + 1) * HEAD_DIM]

    def scores(c):
        return jnp.dot(q_all[...], kt_ref[c], preferred_element_type=F32)

    def absorb(c, s, m, l, acc):
        tiles = [s[:, t * LANES:(t + 1) * LANES] for t in range(lane_tiles)]
        tile_max = functools.reduce(jnp.maximum, tiles)
        m_new = jnp.maximum(m, jnp.max(tile_max, axis=-1, keepdims=True))
        alpha = jnp.exp2(m - m_new)
        p_tiles = [jnp.exp2(t - m_new) for t in tiles]
        l = alpha * l + functools.reduce(jnp.add, p_tiles)
        p = jnp.concatenate(p_tiles, axis=1).astype(BF16)
        start = pl.multiple_of(c * KV_CHUNK, KV_CHUNK)
        pv = jnp.dot(p, v_ref[pl.ds(start, KV_CHUNK), :], preferred_element_type=F32)
        return m_new, l, alpha * acc + pv

    s_ref[...] = scores(0)
    m_ref[...] = jnp.full(m_ref.shape, -jnp.inf, F32)
    l_ref[...] = jnp.zeros(l_ref.shape, F32)
    acc_ref[...] = jnp.zeros(acc_ref.shape, F32)

    def chunk_group(c0, is_last):
        state = (m_ref[...], l_ref[...], acc_ref[...])
        s_cur = s_ref[...]
        for u in range(FLASH_CHUNKS_PER_STEP):
            final_chunk = is_last and u == FLASH_CHUNKS_PER_STEP - 1
            s_next = None if final_chunk else scores(c0 + u + 1)
            state = absorb(c0 + u, s_cur, *state)
            s_cur = s_next
        m_ref[...], l_ref[...], acc_ref[...] = state
        if not is_last:
            s_ref[...] = s_cur

    n_groups = n_chunks // FLASH_CHUNKS_PER_STEP

    def group_body(it, _):
        chunk_group(it * FLASH_CHUNKS_PER_STEP, False)
        return 0

    lax.fori_loop(0, n_groups - 1, group_body, 0)
    chunk_group((n_groups - 1) * FLASH_CHUNKS_PER_STEP, True)

    for g in range(GROUP):
        r = slice(g * tq, (g + 1) * tq)
        l_row = jnp.sum(l_ref[r, :], axis=-1, keepdims=True)
        o_ref[:, g * HEAD_DIM:(g + 1) * HEAD_DIM] = (acc_ref[r, :] / l_row).astype(o_ref.dtype)


def _flash_attention(q, kt, v, block_q=512):
    s = q.shape[0]
    n_chunks = s // KV_CHUNK
    assert n_chunks % FLASH_CHUNKS_PER_STEP == 0
    group_cols = GROUP * HEAD_DIM
    rows = GROUP * block_q
    stat = pltpu.VMEM((rows, LANES), F32)
    est = 4 * _nbytes((block_q, group_cols), BF16) + 4 * _nbytes((s, HEAD_DIM), BF16) \
        + _nbytes((rows, HEAD_DIM), BF16) + _nbytes((rows, KV_CHUNK), F32) \
        + 3 * _nbytes((rows, LANES), F32) + 8 * _nbytes((rows, KV_CHUNK), F32)
    return pl.pallas_call(
        _flash_kernel,
        grid=(N_KV_HEADS, s // block_q),
        in_specs=[pl.BlockSpec((block_q, group_cols), lambda h, i: (i, h)),
                  pl.BlockSpec((n_chunks, HEAD_DIM, KV_CHUNK), lambda h, i: (0, h, 0)),
                  pl.BlockSpec((s, HEAD_DIM), lambda h, i: (0, h))],
        out_specs=pl.BlockSpec((block_q, group_cols), lambda h, i: (i, h)),
        out_shape=jax.ShapeDtypeStruct((s, Q_DIM), BF16),
        scratch_shapes=[pltpu.VMEM((rows, HEAD_DIM), BF16),
                        pltpu.VMEM((rows, KV_CHUNK), F32), stat, stat, stat],
        name="flash_attention",
        compiler_params=_compiler_params(("parallel", "parallel"), est),
    )(q, kt, v)


RGLRU_CHUNK = 512
SCAN_UNROLL = 8
SCAN_BLOCK = SCAN_UNROLL * SUBLANES
CONV_HALO = SUBLANES
N_SEGMENTS = SUBLANES


def _rglru_kernel(rec_ref, gate_ref, cw_ref, cb_ref, gw_ref, gb_ref, lam_ref, y_ref,
                  xpad_ref, af_ref, bf_ref, ab_ref, bb_ref):
    s_len = rec_ref.shape[0]
    seg_len = s_len // N_SEGMENTS
    n_chunks = s_len // RGLRU_CHUNK

    def scan_rows(start):
        seg = start // seg_len
        pos = start - seg * seg_len
        return seg, pl.ds(pos * N_SEGMENTS + seg, RGLRU_CHUNK, stride=N_SEGMENTS)

    zeros_halo = jnp.zeros((CONV_HALO, LANES), F32)
    xpad_ref[0:CONV_HALO, :] = zeros_halo
    xpad_ref[CONV_HALO + s_len:CONV_HALO + s_len + CONV_HALO, :] = zeros_halo
    xpad_ref[CONV_HALO:CONV_HALO + s_len, :] = rec_ref[...]

    neg_lam = -lam_ref[...]
    softplus = jnp.maximum(neg_lam, 0.0) + jnp.log(1.0 + jnp.exp(-jnp.abs(neg_lam)))
    half_decay = (-0.5 * C_DECAY * LOG2_E) * softplus

    def gates_body(c, _):
        start = pl.multiple_of(c * RGLRU_CHUNK, RGLRU_CHUNK)
        _, rows = scan_rows(start)
        xc = cb_ref[...]
        for tap in range(CONV_W):
            off = CONV_HALO - CONV_PAD_L + tap
            xc = xc + xpad_ref[pl.ds(start + off, RGLRU_CHUNK), :] * cw_ref[tap:tap + 1, :]
        t = jnp.tanh(jnp.dot(xc.astype(BF16), gw_ref[...], preferred_element_type=F32)
                     + gb_ref[...])
        xc_half = 0.5 * xc
        for direction, (a_ref, b_ref) in enumerate(((af_ref, bf_ref), (ab_ref, bb_ref))):
            t_r = t[:, (2 * direction) * LANES:(2 * direction + 1) * LANES]
            t_i = t[:, (2 * direction + 1) * LANES:(2 * direction + 2) * LANES]
            hd = half_decay[direction:direction + 1, :]
            a = jnp.exp2(hd * t_r + hd)
            a_ref[rows, :] = a
            b_ref[rows, :] = jnp.sqrt(1.0 - a * a) * (xc_half + xc_half * t_i)
        return 0

    lax.fori_loop(0, n_chunks, gates_body, 0)

    n_iters = s_len // SCAN_BLOCK

    def scan_block(a_ref, b_ref, start, h, prod, reverse):
        rows = pl.ds(start, SCAN_BLOCK)
        a_blk = a_ref[rows, :]
        b_blk = b_ref[rows, :]
        order = range(SCAN_UNROLL - 1, -1, -1) if reverse else range(SCAN_UNROLL)
        h_out = [None] * SCAN_UNROLL
        prod_out = [None] * SCAN_UNROLL
        for u in order:
            a8 = a_blk[u * SUBLANES:(u + 1) * SUBLANES]
            h = a8 * h + b_blk[u * SUBLANES:(u + 1) * SUBLANES]
            prod = a8 * prod
            h_out[u] = h
            prod_out[u] = prod
        b_ref[rows, :] = jnp.concatenate(h_out, axis=0)
        a_ref[rows, :] = jnp.concatenate(prod_out, axis=0)
        return h, prod

    def scan_body(it, carry):
        h_f, p_f, h_b, p_b = carry
        start_f = pl.multiple_of(it * SCAN_BLOCK, SCAN_BLOCK)
        start_b = pl.multiple_of((n_iters - 1 - it) * SCAN_BLOCK, SCAN_BLOCK)
        h_f, p_f = scan_block(af_ref, bf_ref, start_f, h_f, p_f, reverse=False)
        h_b, p_b = scan_block(ab_ref, bb_ref, start_b, h_b, p_b, reverse=True)
        return h_f, p_f, h_b, p_b

    zeros = jnp.zeros((N_SEGMENTS, LANES), F32)
    ones = jnp.ones((N_SEGMENTS, LANES), F32)
    end_f, prod_f, end_b, prod_b = lax.fori_loop(0, n_iters, scan_body, (zeros, ones, zeros, ones))

    def entering_states(end, prod, order):
        state = jnp.zeros((1, LANES), F32)
        entering = [None] * N_SEGMENTS
        for seg in order:
            entering[seg] = state
            state = prod[seg:seg + 1, :] * state + end[seg:seg + 1, :]
        return jnp.concatenate(entering, axis=0)

    enter_f = jnp.tile(entering_states(end_f, prod_f, range(N_SEGMENTS)), (SCAN_UNROLL, 1))
    enter_b = jnp.tile(entering_states(end_b, prod_b, range(N_SEGMENTS - 1, -1, -1)),
                       (SCAN_UNROLL, 1))

    def resolve_body(it, _):
        rows = pl.ds(pl.multiple_of(it * SCAN_BLOCK, SCAN_BLOCK), SCAN_BLOCK)
        bf_ref[rows, :] = (bf_ref[rows, :] + af_ref[rows, :] * enter_f) \
            + (bb_ref[rows, :] + ab_ref[rows, :] * enter_b)
        return 0

    lax.fori_loop(0, n_iters, resolve_body, 0)

    def out_body(c, _):
        start = pl.multiple_of(c * RGLRU_CHUNK, RGLRU_CHUNK)
        _, rows = scan_rows(start)
        y_ref[pl.ds(start, RGLRU_CHUNK), :] = \
            (gate_ref[pl.ds(start, RGLRU_CHUNK), :] * bf_ref[rows, :]).astype(y_ref.dtype)
        return 0

    lax.fori_loop(0, n_chunks, out_body, 0)


def _rglru(rec, gelu_gate, conv_w, conv_b, gate_w_half, gate_b_half, lam):
    s = rec.shape[0]
    col_spec = lambda rows: pl.BlockSpec((rows, LANES), lambda j: (0, j))
    seq_f32 = _nbytes((s, LANES), F32)
    est = 4 * seq_f32 + 2 * _nbytes((s, LANES), BF16) + 5 * seq_f32 \
        + 12 * _nbytes((RGLRU_CHUNK, 4 * LANES), F32)
    return pl.pallas_call(
        _rglru_kernel,
        grid=(N_RNN_BLOCKS,),
        in_specs=[col_spec(s), col_spec(s), col_spec(CONV_W), col_spec(1),
                  pl.BlockSpec((None, RNN_BLOCK, 4 * RNN_BLOCK), lambda j: (j, 0, 0)),
                  pl.BlockSpec((None, 1, 4 * RNN_BLOCK), lambda j: (j, 0, 0)),
                  col_spec(2)],
        out_specs=col_spec(s),
        out_shape=jax.ShapeDtypeStruct((s, D_RNN), BF16),
        scratch_shapes=[pltpu.VMEM((s + 2 * CONV_HALO, LANES), F32)]
        + [pltpu.VMEM((s, LANES), F32)] * 4,
        name="rglru",
        compiler_params=_compiler_params(("parallel",), est),
    )(rec, gelu_gate, conv_w, conv_b.reshape(1, D_RNN), gate_w_half, gate_b_half, lam)


def kernel(x, ffn_norm, ffn_w_gu, ffn_w_down, attn_norm, attn_w_qkv, attn_q_norm, attn_k_norm,
           attn_w_o, rec_norm, rec_w_in, rec_conv_w, rec_conv_b, rec_gate_w, rec_gate_b,
           rec_lambda, rec_w_out, final_norm):
    b, s, d = x.shape
    assert (b, d) == (1, D_MODEL) and s % 1024 == 0
    x = x.reshape(s, d)
    cos_t, sin_t = _rotary_tables(s)
    w_qkv = attn_w_qkv.astype(BF16)
    w_o = attn_w_o.astype(BF16)
    w_out = rec_w_out.astype(BF16)

    def ffn(x, xn, layer, half, next_gain, norm_dtype):
        h, w_down = _dual_matmul(xn, ffn_w_gu, (layer, half), _swiglu_epilogue, (BF16,),
                                 "ffn_gate_up", side=(ffn_w_down, (layer, half)))
        return _proj_residual_norm(h, w_down, (), x, next_gain, 0.5, norm_dtype, block_rows=256)

    xn = _rms_norm(x, ffn_norm[0, 0], BF16)
    for i in range(DEPTH):
        j = i // N_MIXERS
        is_attn = i % N_MIXERS == 0
        x, xn = ffn(x, xn, i, 0, attn_norm[j] if is_attn else rec_norm[j], BF16)
        if is_attn:
            q_scale = (HEAD_DIM ** -0.5) * LOG2_E
            head_gain_cols = jnp.concatenate(
                [jnp.tile(attn_q_norm[j] * q_scale, N_HEADS),
                 jnp.tile(attn_k_norm[j], N_KV_HEADS),
                 jnp.ones((KV_DIM,), F32)]).reshape(1, QKV_DIM)
            q, kt, v = _qkv_proj(xn, w_qkv, (j,), head_gain_cols, cos_t, sin_t)
            y = _flash_attention(q, kt, v)
            w_mix, lead = w_o, (j,)
        else:
            gelu_gate, rec = _dual_matmul(xn, rec_w_in, (j,), _gelu_split_epilogue, (F32, F32),
                                          "rec_in_proj")
            gate_w_half = (0.5 * jnp.transpose(rec_gate_w[j], (2, 3, 0, 1, 4))).reshape(
                N_RNN_BLOCKS, RNN_BLOCK, 4 * RNN_BLOCK).astype(BF16)
            gate_b_half = 0.5 * jnp.transpose(
                rec_gate_b[j].reshape(2, 2, N_RNN_BLOCKS, RNN_BLOCK), (2, 0, 1, 3)).reshape(
                N_RNN_BLOCKS, 1, 4 * RNN_BLOCK)
            y = _rglru(rec, gelu_gate, rec_conv_w[j], rec_conv_b[j], gate_w_half, gate_b_half,
                       rec_lambda[j])
            w_mix, lead = w_out, (j,)
        x, xn = _proj_residual_norm(y, w_mix, lead, x, ffn_norm[i, 1], 1.0, BF16, block_rows=512)
        last = i == DEPTH - 1
        next_gain = final_norm if last else ffn_norm[i + 1, 0]
        x, xn = ffn(x, xn, i, 1, next_gain, F32 if last else BF16)
    return xn.reshape(b, s, d)
```

```python
import functools
import math

import jax
import jax.numpy as jnp
import numpy as np
from jax import lax
from jax.experimental import pallas as pl
from jax.experimental.pallas import tpu as pltpu

D_MODEL = 2048
DEPTH = 4
N_MIXERS = 2
GRID_W = 64
HEAD_DIM = 128
N_HEADS = D_MODEL // HEAD_DIM
N_KV_HEADS = 4
GROUP = N_HEADS // N_KV_HEADS
Q_DIM = N_HEADS * HEAD_DIM
KV_DIM = N_KV_HEADS * HEAD_DIM
QKV_DIM = Q_DIM + 2 * KV_DIM
ROPE_THETA = 10000.0
ROPE_FREQS = HEAD_DIM // 4
D_RNN = D_MODEL
RNN_BLOCK = 128
N_RNN_BLOCKS = D_RNN // RNN_BLOCK
CONV_W = 4
CONV_PAD_L = 2
C_DECAY = 8.0
D_FF = 5632
NORM_EPS = 1e-6

V7X_VMEM_USABLE_BYTES = 58 * 1024 * 1024
SUBLANES = 8
LANES = 128

F32 = jnp.float32
BF16 = jnp.bfloat16
LOG2_E = math.log2(math.e)


def _nbytes(shape, dtype):
    return int(np.prod(shape)) * jnp.dtype(dtype).itemsize


def _compiler_params(semantics, vmem_estimate_bytes):
    limit = min(V7X_VMEM_USABLE_BYTES, max(32 * 1024 * 1024, int(vmem_estimate_bytes)))
    return pltpu.CompilerParams(dimension_semantics=semantics, vmem_limit_bytes=limit)


def _rms_scale(x):
    return lax.rsqrt(jnp.mean(x * x, axis=-1, keepdims=True) + NORM_EPS)


def _stacked_spec(block_shape, lead, index_map):
    return pl.BlockSpec((None,) * len(lead) + tuple(block_shape),
                        lambda *g: tuple(lead) + tuple(index_map(*g)))


def _rms_norm_kernel(x_ref, g_ref, o_ref):
    x = x_ref[...]
    o_ref[...] = (x * _rms_scale(x) * g_ref[...]).astype(o_ref.dtype)


def _rms_norm(x, gain, out_dtype, block_rows=512):
    s, d = x.shape
    est = 2 * (_nbytes((block_rows, d), F32) + _nbytes((block_rows, d), out_dtype)) \
        + 2 * _nbytes((block_rows, d), F32)
    return pl.pallas_call(
        _rms_norm_kernel,
        grid=(s // block_rows,),
        in_specs=[pl.BlockSpec((block_rows, d), lambda i: (i, 0)),
                  pl.BlockSpec((1, d), lambda i: (0, 0))],
        out_specs=pl.BlockSpec((block_rows, d), lambda i: (i, 0)),
        out_shape=jax.ShapeDtypeStruct((s, d), out_dtype),
        name="rms_norm",
        compiler_params=_compiler_params(("parallel",), est),
    )(x, gain.reshape(1, d))


def _swiglu_epilogue(a, b):
    return ((a / (1.0 + jnp.exp(-a))) * b,)


def _gelu_split_epilogue(a, b):
    c = math.sqrt(2.0 / math.pi)
    cdf = 0.5 * (1.0 + jnp.tanh(c * (a + 0.044715 * (a * a * a))))
    return (a * cdf, b)


DUAL_MATMUL_SUB_ROWS = 512


def _dual_matmul_kernel(x_ref, wa_ref, wb_ref, *refs, epilogue, n_sides):
    side_refs = refs[:n_sides]
    out_refs = refs[n_sides:len(refs) - n_sides]
    side_out_refs = refs[len(refs) - n_sides:]
    for side_ref, side_out_ref in zip(side_refs, side_out_refs):
        side_out_ref[...] = side_ref[...].astype(side_out_ref.dtype)
    wa = wa_ref[...].astype(BF16)
    wb = wb_ref[...].astype(BF16)
    for r0 in range(0, x_ref.shape[0], DUAL_MATMUL_SUB_ROWS):
        rows = slice(r0, r0 + DUAL_MATMUL_SUB_ROWS)
        x = x_ref[rows, :]
        a = jnp.dot(x, wa, preferred_element_type=F32)
        b = jnp.dot(x, wb, preferred_element_type=F32)
        for ref, val in zip(out_refs, epilogue(a, b)):
            ref[rows, :] = val.astype(ref.dtype)


def _dual_matmul(x, w, lead, epilogue, out_dtypes, name, sides=(), block_rows=1024,
                 block_cols=512):
    s, k = x.shape
    n = w.shape[-1] // 2
    n_col_blocks = n // block_cols
    n_row_blocks = s // block_rows
    est = 2 * _nbytes((block_rows, k), BF16) + 4 * _nbytes((k, block_cols), w.dtype) \
        + (2 * _nbytes((k, block_cols), BF16) if w.dtype != BF16 else 0) \
        + sum(2 * _nbytes((block_rows, block_cols), dt) for dt in out_dtypes) \
        + 4 * _nbytes((block_rows, block_cols), F32)
    out_spec = pl.BlockSpec((block_rows, block_cols), lambda i, j: (i, j))
    in_specs = [pl.BlockSpec((block_rows, k), lambda i, j: (i, 0)),
                _stacked_spec((k, block_cols), lead, lambda i, j: (0, j)),
                _stacked_spec((k, block_cols), lead, lambda i, j: (0, j + n_col_blocks))]
    out_specs = [out_spec] * len(out_dtypes)
    out_shape = [jax.ShapeDtypeStruct((s, n), dt) for dt in out_dtypes]
    operands = [x, w, w]
    n_steps = n_row_blocks * n_col_blocks
    for side_w, side_lead, slab_axis in sides:
        side_shape = side_w.shape[-2:]
        slab, rem = divmod(side_shape[slab_axis], n_steps)
        assert rem == 0 and slab % ((2 * SUBLANES, LANES)[slab_axis]) == 0
        slab_shape = tuple(slab if ax == slab_axis else dim for ax, dim in enumerate(side_shape))

        def slab_index(i, j, slab_axis=slab_axis):
            step = i * n_col_blocks + j
            return (step, 0) if slab_axis == 0 else (0, step)

        in_specs.append(_stacked_spec(slab_shape, side_lead, slab_index))
        out_specs.append(pl.BlockSpec(slab_shape, slab_index))
        out_shape.append(jax.ShapeDtypeStruct(side_shape, BF16))
        operands.append(side_w)
        est += 2 * _nbytes(slab_shape, F32) + 2 * _nbytes(slab_shape, BF16)
    return pl.pallas_call(
        functools.partial(_dual_matmul_kernel, epilogue=epilogue, n_sides=len(sides)),
        grid=(n_row_blocks, n_col_blocks),
        in_specs=in_specs,
        out_specs=out_specs,
        out_shape=out_shape,
        name=name,
        compiler_params=_compiler_params(("parallel", "arbitrary"), est),
    )(*operands)


def _proj_residual_norm_kernel(y_ref, w_ref, x_ref, g_ref, xo_ref, xn_ref, *, branch_scale):
    acc = jnp.dot(y_ref[...], w_ref[...], preferred_element_type=F32)
    x_new = x_ref[...] + branch_scale * acc
    xo_ref[...] = x_new
    xn_ref[...] = (x_new * _rms_scale(x_new) * g_ref[...]).astype(xn_ref.dtype)


def _proj_residual_norm(y, w, lead, x, next_gain, branch_scale, norm_dtype, block_rows):
    s, k = y.shape
    d = w.shape[-1]
    est = _nbytes((k, d), BF16) + 2 * _nbytes((block_rows, k), BF16) \
        + 4 * _nbytes((block_rows, d), F32) + 2 * _nbytes((block_rows, d), norm_dtype) \
        + 3 * _nbytes((block_rows, d), F32)
    row_spec = lambda cols: pl.BlockSpec((block_rows, cols), lambda i: (i, 0))
    w_spec = pl.BlockSpec((None,) * len(lead) + (k, d), lambda i: tuple(lead) + (0, 0),
                          pipeline_mode=pl.Buffered(1))
    return pl.pallas_call(
        functools.partial(_proj_residual_norm_kernel, branch_scale=branch_scale),
        grid=(s // block_rows,),
        in_specs=[row_spec(k), w_spec, row_spec(d), pl.BlockSpec((1, d), lambda i: (0, 0))],
        out_specs=[row_spec(d), row_spec(d)],
        out_shape=[jax.ShapeDtypeStruct((s, d), F32), jax.ShapeDtypeStruct((s, d), norm_dtype)],
        name=f"proj_residual_norm_k{k}",
        compiler_params=_compiler_params(("parallel",), est),
    )(y, w, x, next_gain.reshape(1, d))


KV_CHUNK = 512
QKV_COLS_PER_DOT = 512


def _swap_rotary_halves(x):
    lane = lax.broadcasted_iota(jnp.int32, x.shape, 1)
    first_quarter = (lane % (2 * ROPE_FREQS)) < ROPE_FREQS
    return jnp.where(first_quarter,
                     pltpu.roll(x, HEAD_DIM - ROPE_FREQS, 1),
                     pltpu.roll(x, ROPE_FREQS, 1))


def _qkv_kernel(x_ref, w_ref, gain_ref, cos_ref, sin_ref, q_ref, kt_ref, v_ref):
    x = x_ref[...]
    cos_t = cos_ref[...]
    sin_t = sin_ref[...]
    heads_per_dot = QKV_COLS_PER_DOT // HEAD_DIM

    def normed_rotated(acc, h, col0):
        cols = slice(col0 + h * HEAD_DIM, col0 + (h + 1) * HEAD_DIM)
        xh = acc[:, h * HEAD_DIM:(h + 1) * HEAD_DIM]
        xh = xh * _rms_scale(xh) * gain_ref[:, cols]
        return xh * cos_t + _swap_rotary_halves(xh) * sin_t

    for jb in range(QKV_DIM // QKV_COLS_PER_DOT):
        col0 = jb * QKV_COLS_PER_DOT
        acc = jnp.dot(x, w_ref[:, col0:col0 + QKV_COLS_PER_DOT], preferred_element_type=F32)
        if col0 < Q_DIM:
            for h in range(heads_per_dot):
                q_ref[:, col0 + h * HEAD_DIM:col0 + (h + 1) * HEAD_DIM] = \
                    normed_rotated(acc, h, col0).astype(q_ref.dtype)
        elif col0 < Q_DIM + KV_DIM:
            for h in range(heads_per_dot):
                row0 = col0 - Q_DIM + h * HEAD_DIM
                kt_ref[row0:row0 + HEAD_DIM, :] = \
                    normed_rotated(acc, h, col0).T.astype(kt_ref.dtype)
        else:
            v0 = col0 - Q_DIM - KV_DIM
            v_ref[:, v0:v0 + QKV_COLS_PER_DOT] = acc.astype(v_ref.dtype)


def _qkv_proj(xn, w_qkv, lead, head_gain_cols, cos_t, sin_t):
    s, k = xn.shape
    rows = KV_CHUNK
    est = _nbytes((k, QKV_DIM), BF16) + 2 * _nbytes((rows, k), BF16) \
        + 4 * _nbytes((rows, HEAD_DIM), F32) + 2 * _nbytes((rows, QKV_DIM), BF16) \
        + 6 * _nbytes((rows, QKV_COLS_PER_DOT), F32)
    return pl.pallas_call(
        _qkv_kernel,
        grid=(s // rows,),
        in_specs=[pl.BlockSpec((rows, k), lambda i: (i, 0)),
                  pl.BlockSpec((None,) * len(lead) + (k, QKV_DIM),
                               lambda i: tuple(lead) + (0, 0), pipeline_mode=pl.Buffered(1)),
                  pl.BlockSpec((1, QKV_DIM), lambda i: (0, 0)),
                  pl.BlockSpec((rows, HEAD_DIM), lambda i: (i, 0)),
                  pl.BlockSpec((rows, HEAD_DIM), lambda i: (i, 0))],
        out_specs=[pl.BlockSpec((rows, Q_DIM), lambda i: (i, 0)),
                   pl.BlockSpec((None, KV_DIM, rows), lambda i: (i, 0, 0)),
                   pl.BlockSpec((rows, KV_DIM), lambda i: (i, 0))],
        out_shape=[jax.ShapeDtypeStruct((s, Q_DIM), BF16),
                   jax.ShapeDtypeStruct((s // rows, KV_DIM, rows), BF16),
                   jax.ShapeDtypeStruct((s, KV_DIM), BF16)],
        name="qkv_proj",
        compiler_params=_compiler_params(("parallel",), est),
    )(xn, w_qkv, head_gain_cols, cos_t, sin_t)


def _rotary_tables(s):
    n_rows = s // GRID_W
    inv_freq = ROPE_THETA ** (-jnp.arange(ROPE_FREQS, dtype=F32) / ROPE_FREQS)
    ang_row = jnp.arange(n_rows, dtype=F32)[:, None] * inv_freq[None, :]
    ang_col = jnp.arange(GRID_W, dtype=F32)[:, None] * inv_freq[None, :]
    by_row = lambda t: jnp.repeat(t, GRID_W, axis=0)
    by_col = lambda t: jnp.tile(t, (n_rows, 1))
    cos_r, sin_r = by_row(jnp.cos(ang_row)), by_row(jnp.sin(ang_row))
    cos_c, sin_c = by_col(jnp.cos(ang_col)), by_col(jnp.sin(ang_col))
    cos_t = jnp.concatenate([cos_r, cos_r, cos_c, cos_c], axis=-1)
    sin_t = jnp.concatenate([-sin_r, sin_r, -sin_c, sin_c], axis=-1)
    return cos_t, sin_t


FLASH_CHUNKS_PER_STEP = 8


def _flash_kernel(q_ref, kt_ref, v_ref, o_ref, q_all, s_ref, m_ref, l_ref, acc_ref):
    tq = q_ref.shape[0]
    n_chunks = kt_ref.shape[0]
    lane_tiles = KV_CHUNK // LANES

    for g in range(GROUP):
        q_all[g * tq:(g + 1) * tq, :] = q_ref[:, g * HEAD_DIM:(g + 1) * HEAD_DIM]

    def scores(c):
        return jnp.dot(q_all[...], kt_ref[c], preferred_element_type=F32)

    def absorb(c, s, m, l, acc):
        tiles = [s[:, t * LANES:(t + 1) * LANES] for t in range(lane_tiles)]
        tile_max = functools.reduce(jnp.maximum, tiles)
        m_new = jnp.maximum(m, jnp.max(tile_max, axis=-1, keepdims=True))
        alpha = jnp.exp2(m - m_new)
        p_tiles = [jnp.exp2(t - m_new) for t in tiles]
        l = alpha * l + functools.reduce(jnp.add, p_tiles)
        p = jnp.concatenate(p_tiles, axis=1).astype(BF16)
        start = pl.multiple_of(c * KV_CHUNK, KV_CHUNK)
        pv = jnp.dot(p, v_ref[pl.ds(start, KV_CHUNK), :], preferred_element_type=F32)
        return m_new, l, alpha * acc + pv

    s_ref[...] = scores(0)
    m_ref[...] = jnp.full(m_ref.shape, -jnp.inf, F32)
    l_ref[...] = jnp.zeros(l_ref.shape, F32)
    acc_ref[...] = jnp.zeros(acc_ref.shape, F32)

    def chunk_group(c0, is_last):
        state = (m_ref[...], l_ref[...], acc_ref[...])
        s_cur = s_ref[...]
        for u in range(FLASH_CHUNKS_PER_STEP):
            final_chunk = is_last and u == FLASH_CHUNKS_PER_STEP - 1
            s_next = None if final_chunk else scores(c0 + u + 1)
            state = absorb(c0 + u, s_cur, *state)
            s_cur = s_next
        m_ref[...], l_ref[...], acc_ref[...] = state
        if not is_last:
            s_ref[...] = s_cur

    n_groups = n_chunks // FLASH_CHUNKS_PER_STEP

    def group_body(it, _):
        chunk_group(it * FLASH_CHUNKS_PER_STEP, False)
        return 0

    lax.fori_loop(0, n_groups - 1, group_body, 0)
    chunk_group((n_groups - 1) * FLASH_CHUNKS_PER_STEP, True)

    for g in range(GROUP):
        r = slice(g * tq, (g + 1) * tq)
        l_row = jnp.sum(l_ref[r, :], axis=-1, keepdims=True)
        o_ref[:, g * HEAD_DIM:(g + 1) * HEAD_DIM] = (acc_ref[r, :] / l_row).astype(o_ref.dtype)


def _flash_attention(q, kt, v, block_q=256):
    s = q.shape[0]
    n_chunks = s // KV_CHUNK
    assert n_chunks % FLASH_CHUNKS_PER_STEP == 0
    group_cols = GROUP * HEAD_DIM
    rows = GROUP * block_q
    stat = pltpu.VMEM((rows, LANES), F32)
    est = 4 * _nbytes((block_q, group_cols), BF16) + 4 * _nbytes((s, HEAD_DIM), BF16) \
        + _nbytes((rows, HEAD_DIM), BF16) + _nbytes((rows, KV_CHUNK), F32) \
        + 3 * _nbytes((rows, LANES), F32) + 8 * _nbytes((rows, KV_CHUNK), F32)
    return pl.pallas_call(
        _flash_kernel,
        grid=(N_KV_HEADS, s // block_q),
        in_specs=[pl.BlockSpec((block_q, group_cols), lambda h, i: (i, h)),
                  pl.BlockSpec((n_chunks, HEAD_DIM, KV_CHUNK), lambda h, i: (0, h, 0)),
                  pl.BlockSpec((s, HEAD_DIM), lambda h, i: (0, h))],
        out_specs=pl.BlockSpec((block_q, group_cols), lambda h, i: (i, h)),
        out_shape=jax.ShapeDtypeStruct((s, Q_DIM), BF16),
        scratch_shapes=[pltpu.VMEM((rows, HEAD_DIM), BF16),
                        pltpu.VMEM((rows, KV_CHUNK), F32), stat, stat, stat],
        name="flash_attention",
        compiler_params=_compiler_params(("parallel", "parallel"), est),
    )(q, kt, v)


RGLRU_CHUNK = 512
SCAN_UNROLL = 8
SCAN_BLOCK = SCAN_UNROLL * SUBLANES
CONV_HALO = SUBLANES
N_SEGMENTS = SUBLANES


def _rglru_kernel(rec_ref, gate_ref, cw_ref, cb_ref, gw_ref, gb_ref, lam_ref, y_ref,
                  xpad_ref, af_ref, bf_ref, ab_ref, bb_ref):
    s_len = rec_ref.shape[0]
    seg_len = s_len // N_SEGMENTS
    n_chunks = s_len // RGLRU_CHUNK

    def scan_rows(start):
        seg = start // seg_len
        pos = start - seg * seg_len
        return seg, pl.ds(pos * N_SEGMENTS + seg, RGLRU_CHUNK, stride=N_SEGMENTS)

    zeros_halo = jnp.zeros((CONV_HALO, LANES), F32)
    xpad_ref[0:CONV_HALO, :] = zeros_halo
    xpad_ref[CONV_HALO + s_len:CONV_HALO + s_len + CONV_HALO, :] = zeros_halo
    xpad_ref[CONV_HALO:CONV_HALO + s_len, :] = rec_ref[...]

    neg_lam = -lam_ref[...]
    softplus = jnp.maximum(neg_lam, 0.0) + jnp.log(1.0 + jnp.exp(-jnp.abs(neg_lam)))
    half_decay = (-0.5 * C_DECAY * LOG2_E) * softplus

    def gates_body(c, _):
        start = pl.multiple_of(c * RGLRU_CHUNK, RGLRU_CHUNK)
        _, rows = scan_rows(start)
        xc = cb_ref[...]
        for tap in range(CONV_W):
            off = CONV_HALO - CONV_PAD_L + tap
            xc = xc + xpad_ref[pl.ds(start + off, RGLRU_CHUNK), :] * cw_ref[tap:tap + 1, :]
        t = jnp.tanh(jnp.dot(xc.astype(BF16), gw_ref[...], preferred_element_type=F32)
                     + gb_ref[...])
        xc_half = 0.5 * xc
        for direction, (a_ref, b_ref) in enumerate(((af_ref, bf_ref), (ab_ref, bb_ref))):
            t_r = t[:, (2 * direction) * LANES:(2 * direction + 1) * LANES]
            t_i = t[:, (2 * direction + 1) * LANES:(2 * direction + 2) * LANES]
            hd = half_decay[direction:direction + 1, :]
            a = jnp.exp2(hd * t_r + hd)
            a_ref[rows, :] = a
            b_ref[rows, :] = jnp.sqrt(1.0 - a * a) * (xc_half + xc_half * t_i)
        return 0

    lax.fori_loop(0, n_chunks, gates_body, 0)

    n_iters = s_len // SCAN_BLOCK

    def scan_block(a_ref, b_ref, start, h, prod, reverse):
        rows = pl.ds(start, SCAN_BLOCK)
        a_blk = a_ref[rows, :]
        b_blk = b_ref[rows, :]
        order = range(SCAN_UNROLL - 1, -1, -1) if reverse else range(SCAN_UNROLL)
        h_out = [None] * SCAN_UNROLL
        prod_out = [None] * SCAN_UNROLL
        for u in order:
            a8 = a_blk[u * SUBLANES:(u + 1) * SUBLANES]
            h = a8 * h + b_blk[u * SUBLANES:(u + 1) * SUBLANES]
            prod = a8 * prod
            h_out[u] = h
            prod_out[u] = prod
        b_ref[rows, :] = jnp.concatenate(h_out, axis=0)
        a_ref[rows, :] = jnp.concatenate(prod_out, axis=0)
        return h, prod

    def scan_body(it, carry):
        h_f, p_f, h_b, p_b = carry
        start_f = pl.multiple_of(it * SCAN_BLOCK, SCAN_BLOCK)
        start_b = pl.multiple_of((n_iters - 1 - it) * SCAN_BLOCK, SCAN_BLOCK)
        h_f, p_f = scan_block(af_ref, bf_ref, start_f, h_f, p_f, reverse=False)
        h_b, p_b = scan_block(ab_ref, bb_ref, start_b, h_b, p_b, reverse=True)
        return h_f, p_f, h_b, p_b

    zeros = jnp.zeros((N_SEGMENTS, LANES), F32)
    ones = jnp.ones((N_SEGMENTS, LANES), F32)
    end_f, prod_f, end_b, prod_b = lax.fori_loop(0, n_iters, scan_body, (zeros, ones, zeros, ones))

    def entering_states(end, prod, order):
        state = jnp.zeros((1, LANES), F32)
        entering = [None] * N_SEGMENTS
        for seg in order:
            entering[seg] = state
            state = prod[seg:seg + 1, :] * state + end[seg:seg + 1, :]
        return jnp.concatenate(entering, axis=0)

    enter_f = jnp.tile(entering_states(end_f, prod_f, range(N_SEGMENTS)), (SCAN_UNROLL, 1))
    enter_b = jnp.tile(entering_states(end_b, prod_b, range(N_SEGMENTS - 1, -1, -1)),
                       (SCAN_UNROLL, 1))

    def resolve_body(it, _):
        rows = pl.ds(pl.multiple_of(it * SCAN_BLOCK, SCAN_BLOCK), SCAN_BLOCK)
        bf_ref[rows, :] = (bf_ref[rows, :] + af_ref[rows, :] * enter_f) \
            + (bb_ref[rows, :] + ab_ref[rows, :] * enter_b)
        return 0

    lax.fori_loop(0, n_iters, resolve_body, 0)

    def out_body(c, _):
        start = pl.multiple_of(c * RGLRU_CHUNK, RGLRU_CHUNK)
        _, rows = scan_rows(start)
        y_ref[pl.ds(start, RGLRU_CHUNK), :] = \
            (gate_ref[pl.ds(start, RGLRU_CHUNK), :] * bf_ref[rows, :]).astype(y_ref.dtype)
        return 0

    lax.fori_loop(0, n_chunks, out_body, 0)


def _rglru(rec, gelu_gate, conv_w, conv_b, gate_w_half, gate_b_half, lam):
    s = rec.shape[0]
    col_spec = lambda rows: pl.BlockSpec((rows, LANES), lambda j: (0, j))
    seq_f32 = _nbytes((s, LANES), F32)
    est = 4 * seq_f32 + 2 * _nbytes((s, LANES), BF16) + 5 * seq_f32 \
        + 12 * _nbytes((RGLRU_CHUNK, 4 * LANES), F32)
    return pl.pallas_call(
        _rglru_kernel,
        grid=(N_RNN_BLOCKS,),
        in_specs=[col_spec(s), col_spec(s), col_spec(CONV_W), col_spec(1),
                  pl.BlockSpec((None, RNN_BLOCK, 4 * RNN_BLOCK), lambda j: (j, 0, 0)),
                  pl.BlockSpec((None, 1, 4 * RNN_BLOCK), lambda j: (j, 0, 0)),
                  col_spec(2)],
        out_specs=col_spec(s),
        out_shape=jax.ShapeDtypeStruct((s, D_RNN), BF16),
        scratch_shapes=[pltpu.VMEM((s + 2 * CONV_HALO, LANES), F32)]
        + [pltpu.VMEM((s, LANES), F32)] * 4,
        name="rglru",
        compiler_params=_compiler_params(("parallel",), est),
    )(rec, gelu_gate, conv_w, conv_b.reshape(1, D_RNN), gate_w_half, gate_b_half, lam)


def kernel(x, ffn_norm, ffn_w_gu, ffn_w_down, attn_norm, attn_w_qkv, attn_q_norm, attn_k_norm,
           attn_w_o, rec_norm, rec_w_in, rec_conv_w, rec_conv_b, rec_gate_w, rec_gate_b,
           rec_lambda, rec_w_out, final_norm):
    b, s, d = x.shape
    assert (b, d) == (1, D_MODEL) and s % 1024 == 0
    x = x.reshape(s, d)
    cos_t, sin_t = _rotary_tables(s)
    w_qkv = attn_w_qkv.astype(BF16)
    w_o = attn_w_o.astype(BF16)

    ffn_order = [(layer, half) for layer in range(DEPTH) for half in range(2)]
    w_gu_bf16 = {ffn_order[0]: ffn_w_gu[0, 0].astype(BF16)}

    def ffn(x, xn, layer, half, next_gain, norm_dtype):
        sides = [(ffn_w_down, (layer, half), 0)]
        position = ffn_order.index((layer, half))
        if position + 1 < len(ffn_order):
            sides.append((ffn_w_gu, ffn_order[position + 1], 1))
        h, w_down, *w_gu_next = _dual_matmul(
            xn, w_gu_bf16.pop((layer, half)), (), _swiglu_epilogue, (BF16,), "ffn_gate_up",
            sides=sides, block_rows=2048)
        if w_gu_next:
            w_gu_bf16[ffn_order[position + 1]] = w_gu_next[0]
        return _proj_residual_norm(h, w_down, (), x, next_gain, 0.5, norm_dtype, block_rows=256)

    xn = _rms_norm(x, ffn_norm[0, 0], BF16)
    for i in range(DEPTH):
        j = i // N_MIXERS
        is_attn = i % N_MIXERS == 0
        x, xn = ffn(x, xn, i, 0, attn_norm[j] if is_attn else rec_norm[j], BF16)
        if is_attn:
            q_scale = (HEAD_DIM ** -0.5) * LOG2_E
            head_gain_cols = jnp.concatenate(
                [jnp.tile(attn_q_norm[j] * q_scale, N_HEADS),
                 jnp.tile(attn_k_norm[j], N_KV_HEADS),
                 jnp.ones((KV_DIM,), F32)]).reshape(1, QKV_DIM)
            q, kt, v = _qkv_proj(xn, w_qkv, (j,), head_gain_cols, cos_t, sin_t)
            y = _flash_attention(q, kt, v)
            w_mix, lead = w_o, (j,)
        else:
            gelu_gate, rec, w_out = _dual_matmul(xn, rec_w_in, (j,), _gelu_split_epilogue,
                                                 (F32, F32), "rec_in_proj",
                                                 sides=[(rec_w_out, (j,), 0)])
            gate_w_half = (0.5 * jnp.transpose(rec_gate_w[j], (2, 3, 0, 1, 4))).reshape(
                N_RNN_BLOCKS, RNN_BLOCK, 4 * RNN_BLOCK).astype(BF16)
            gate_b_half = 0.5 * jnp.transpose(
                rec_gate_b[j].reshape(2, 2, N_RNN_BLOCKS, RNN_BLOCK), (2, 0, 1, 3)).reshape(
                N_RNN_BLOCKS, 1, 4 * RNN_BLOCK)
            y = _rglru(rec, gelu_gate, rec_conv_w[j], rec_conv_b[j], gate_w_half, gate_b_half,
                       rec_lambda[j])
            w_mix, lead = w_out, ()
        x, xn = _proj_residual_norm(y, w_mix, lead, x, ffn_norm[i, 1], 1.0, BF16, block_rows=512)
        last = i == DEPTH - 1
        next_gain = final_norm if last else ffn_norm[i + 1, 0]
        x, xn = ffn(x, xn, i, 1, next_gain, F32 if last else BF16)
    return xn.reshape(b, s, d)
```

```python
import functools
import math

import jax
import jax.numpy as jnp
import numpy as np
from jax import lax
from jax.experimental import pallas as pl
from jax.experimental.pallas import tpu as pltpu

D_MODEL = 2048
DEPTH = 4
N_MIXERS = 2
GRID_W = 64
HEAD_DIM = 128
N_HEADS = D_MODEL // HEAD_DIM
N_KV_HEADS = 4
GROUP = N_HEADS // N_KV_HEADS
Q_DIM = N_HEADS * HEAD_DIM
KV_DIM = N_KV_HEADS * HEAD_DIM
QKV_DIM = Q_DIM + 2 * KV_DIM
ROPE_THETA = 10000.0
ROPE_FREQS = HEAD_DIM // 4
D_RNN = D_MODEL
RNN_BLOCK = 128
N_RNN_BLOCKS = D_RNN // RNN_BLOCK
CONV_W = 4
CONV_PAD_L = 2
C_DECAY = 8.0
D_FF = 5632
NORM_EPS = 1e-6

V7X_VMEM_USABLE_BYTES = 58 * 1024 * 1024
SUBLANES = 8
LANES = 128

F32 = jnp.float32
BF16 = jnp.bfloat16
LOG2_E = math.log2(math.e)


def _nbytes(shape, dtype):
    return int(np.prod(shape)) * jnp.dtype(dtype).itemsize


def _compiler_params(semantics, vmem_estimate_bytes):
    limit = min(V7X_VMEM_USABLE_BYTES, max(32 * 1024 * 1024, int(vmem_estimate_bytes)))
    return pltpu.CompilerParams(dimension_semantics=semantics, vmem_limit_bytes=limit)


def _rms_scale(x):
    return lax.rsqrt(jnp.mean(x * x, axis=-1, keepdims=True) + NORM_EPS)


def _stacked_spec(block_shape, lead, index_map):
    return pl.BlockSpec((None,) * len(lead) + tuple(block_shape),
                        lambda *g: tuple(lead) + tuple(index_map(*g)))


def _rms_norm_kernel(x_ref, g_ref, o_ref):
    x = x_ref[...]
    o_ref[...] = (x * _rms_scale(x) * g_ref[...]).astype(o_ref.dtype)


def _rms_norm(x, gain, out_dtype, block_rows=512):
    s, d = x.shape
    est = 2 * (_nbytes((block_rows, d), F32) + _nbytes((block_rows, d), out_dtype)) \
        + 2 * _nbytes((block_rows, d), F32)
    return pl.pallas_call(
        _rms_norm_kernel,
        grid=(s // block_rows,),
        in_specs=[pl.BlockSpec((block_rows, d), lambda i: (i, 0)),
                  pl.BlockSpec((1, d), lambda i: (0, 0))],
        out_specs=pl.BlockSpec((block_rows, d), lambda i: (i, 0)),
        out_shape=jax.ShapeDtypeStruct((s, d), out_dtype),
        name="rms_norm",
        compiler_params=_compiler_params(("parallel",), est),
    )(x, gain.reshape(1, d))


def _swiglu_epilogue(a, b):
    return ((a / (1.0 + jnp.exp(-a))) * b,)


def _gelu_split_epilogue(a, b):
    c = math.sqrt(2.0 / math.pi)
    cdf = 0.5 * (1.0 + jnp.tanh(c * (a + 0.044715 * (a * a * a))))
    return (a * cdf, b)


DUAL_MATMUL_SUB_ROWS = 512


def _dual_matmul_kernel(x_ref, wa_ref, wb_ref, *refs, epilogue, n_sides):
    side_refs = refs[:n_sides]
    out_refs = refs[n_sides:len(refs) - n_sides]
    side_out_refs = refs[len(refs) - n_sides:]
    for side_ref, side_out_ref in zip(side_refs, side_out_refs):
        side_out_ref[...] = side_ref[...].astype(side_out_ref.dtype)
    wa = wa_ref[...].astype(BF16)
    wb = wb_ref[...].astype(BF16)
    for r0 in range(0, x_ref.shape[0], DUAL_MATMUL_SUB_ROWS):
        rows = slice(r0, r0 + DUAL_MATMUL_SUB_ROWS)
        x = x_ref[rows, :]
        a = jnp.dot(x, wa, preferred_element_type=F32)
        b = jnp.dot(x, wb, preferred_element_type=F32)
        for ref, val in zip(out_refs, epilogue(a, b)):
            ref[rows, :] = val.astype(ref.dtype)


def _dual_matmul(x, w, lead, epilogue, out_dtypes, name, sides=(), block_rows=1024,
                 block_cols=512):
    s, k = x.shape
    n = w.shape[-1] // 2
    n_col_blocks = n // block_cols
    n_row_blocks = s // block_rows
    est = 2 * _nbytes((block_rows, k), BF16) + 4 * _nbytes((k, block_cols), w.dtype) \
        + (2 * _nbytes((k, block_cols), BF16) if w.dtype != BF16 else 0) \
        + sum(2 * _nbytes((block_rows, block_cols), dt) for dt in out_dtypes) \
        + 4 * _nbytes((block_rows, block_cols), F32)
    out_spec = pl.BlockSpec((block_rows, block_cols), lambda i, j: (i, j))
    in_specs = [pl.BlockSpec((block_rows, k), lambda i, j: (i, 0)),
                _stacked_spec((k, block_cols), lead, lambda i, j: (0, j)),
                _stacked_spec((k, block_cols), lead, lambda i, j: (0, j + n_col_blocks))]
    out_specs = [out_spec] * len(out_dtypes)
    out_shape = [jax.ShapeDtypeStruct((s, n), dt) for dt in out_dtypes]
    operands = [x, w, w]
    n_steps = n_row_blocks * n_col_blocks
    for side_w, side_lead, slab_axis in sides:
        side_shape = side_w.shape[-2:]
        slab, rem = divmod(side_shape[slab_axis], n_steps)
        assert rem == 0 and slab % ((2 * SUBLANES, LANES)[slab_axis]) == 0
        slab_shape = tuple(slab if ax == slab_axis else dim for ax, dim in enumerate(side_shape))

        def slab_index(i, j, slab_axis=slab_axis):
            step = i * n_col_blocks + j
            return (step, 0) if slab_axis == 0 else (0, step)

        in_specs.append(_stacked_spec(slab_shape, side_lead, slab_index))
        out_specs.append(pl.BlockSpec(slab_shape, slab_index))
        out_shape.append(jax.ShapeDtypeStruct(side_shape, BF16))
        operands.append(side_w)
        est += 2 * _nbytes(slab_shape, F32) + 2 * _nbytes(slab_shape, BF16)
    return pl.pallas_call(
        functools.partial(_dual_matmul_kernel, epilogue=epilogue, n_sides=len(sides)),
        grid=(n_row_blocks, n_col_blocks),
        in_specs=in_specs,
        out_specs=out_specs,
        out_shape=out_shape,
        name=name,
        compiler_params=_compiler_params(("parallel", "arbitrary"), est),
    )(*operands)


def _proj_residual_norm_kernel(y_ref, w_ref, x_ref, g_ref, xo_ref, xn_ref, *, branch_scale):
    acc = jnp.dot(y_ref[...], w_ref[...], preferred_element_type=F32)
    x_new = x_ref[...] + branch_scale * acc
    xo_ref[...] = x_new
    xn_ref[...] = (x_new * _rms_scale(x_new) * g_ref[...]).astype(xn_ref.dtype)


def _proj_residual_norm(y, w, lead, x, next_gain, branch_scale, norm_dtype, block_rows):
    s, k = y.shape
    d = w.shape[-1]
    est = _nbytes((k, d), BF16) + 2 * _nbytes((block_rows, k), BF16) \
        + 4 * _nbytes((block_rows, d), F32) + 2 * _nbytes((block_rows, d), norm_dtype) \
        + 3 * _nbytes((block_rows, d), F32)
    row_spec = lambda cols: pl.BlockSpec((block_rows, cols), lambda i: (i, 0))
    w_spec = pl.BlockSpec((None,) * len(lead) + (k, d), lambda i: tuple(lead) + (0, 0),
                          pipeline_mode=pl.Buffered(1))
    return pl.pallas_call(
        functools.partial(_proj_residual_norm_kernel, branch_scale=branch_scale),
        grid=(s // block_rows,),
        in_specs=[row_spec(k), w_spec, row_spec(d), pl.BlockSpec((1, d), lambda i: (0, 0))],
        out_specs=[row_spec(d), row_spec(d)],
        out_shape=[jax.ShapeDtypeStruct((s, d), F32), jax.ShapeDtypeStruct((s, d), norm_dtype)],
        name=f"proj_residual_norm_k{k}",
        compiler_params=_compiler_params(("parallel",), est),
    )(y, w, x, next_gain.reshape(1, d))


KV_CHUNK = 512
QKV_COLS_PER_DOT = 512


def _swap_rotary_halves(x):
    lane = lax.broadcasted_iota(jnp.int32, x.shape, 1)
    first_quarter = (lane % (2 * ROPE_FREQS)) < ROPE_FREQS
    return jnp.where(first_quarter,
                     pltpu.roll(x, HEAD_DIM - ROPE_FREQS, 1),
                     pltpu.roll(x, ROPE_FREQS, 1))


def _qkv_kernel(x_ref, w_ref, gain_ref, cos_ref, sin_ref, q_ref, kt_ref, v_ref):
    x = x_ref[...]
    cos_t = cos_ref[...]
    sin_t = sin_ref[...]
    heads_per_dot = QKV_COLS_PER_DOT // HEAD_DIM

    def normed_rotated(acc, h, col0):
        cols = slice(col0 + h * HEAD_DIM, col0 + (h + 1) * HEAD_DIM)
        xh = acc[:, h * HEAD_DIM:(h + 1) * HEAD_DIM]
        xh = xh * _rms_scale(xh) * gain_ref[:, cols]
        return xh * cos_t + _swap_rotary_halves(xh) * sin_t

    for jb in range(QKV_DIM // QKV_COLS_PER_DOT):
        col0 = jb * QKV_COLS_PER_DOT
        acc = jnp.dot(x, w_ref[:, col0:col0 + QKV_COLS_PER_DOT], preferred_element_type=F32)
        if col0 < Q_DIM:
            for h in range(heads_per_dot):
                q_ref[:, col0 + h * HEAD_DIM:col0 + (h + 1) * HEAD_DIM] = \
                    normed_rotated(acc, h, col0).astype(q_ref.dtype)
        elif col0 < Q_DIM + KV_DIM:
            for h in range(heads_per_dot):
                row0 = col0 - Q_DIM + h * HEAD_DIM
                kt_ref[row0:row0 + HEAD_DIM, :] = \
                    normed_rotated(acc, h, col0).T.astype(kt_ref.dtype)
        else:
            v0 = col0 - Q_DIM - KV_DIM
            v_ref[:, v0:v0 + QKV_COLS_PER_DOT] = acc.astype(v_ref.dtype)


def _qkv_proj(xn, w_qkv, lead, head_gain_cols, cos_t, sin_t):
    s, k = xn.shape
    rows = KV_CHUNK
    est = _nbytes((k, QKV_DIM), BF16) + 2 * _nbytes((rows, k), BF16) \
        + 4 * _nbytes((rows, HEAD_DIM), F32) + 2 * _nbytes((rows, QKV_DIM), BF16) \
        + 6 * _nbytes((rows, QKV_COLS_PER_DOT), F32)
    return pl.pallas_call(
        _qkv_kernel,
        grid=(s // rows,),
        in_specs=[pl.BlockSpec((rows, k), lambda i: (i, 0)),
                  pl.BlockSpec((None,) * len(lead) + (k, QKV_DIM),
                               lambda i: tuple(lead) + (0, 0), pipeline_mode=pl.Buffered(1)),
                  pl.BlockSpec((1, QKV_DIM), lambda i: (0, 0)),
                  pl.BlockSpec((rows, HEAD_DIM), lambda i: (i, 0)),
                  pl.BlockSpec((rows, HEAD_DIM), lambda i: (i, 0))],
        out_specs=[pl.BlockSpec((rows, Q_DIM), lambda i: (i, 0)),
                   pl.BlockSpec((None, KV_DIM, rows), lambda i: (i, 0, 0)),
                   pl.BlockSpec((rows, KV_DIM), lambda i: (i, 0))],
        out_shape=[jax.ShapeDtypeStruct((s, Q_DIM), BF16),
                   jax.ShapeDtypeStruct((s // rows, KV_DIM, rows), BF16),
                   jax.ShapeDtypeStruct((s, KV_DIM), BF16)],
        name="qkv_proj",
        compiler_params=_compiler_params(("parallel",), est),
    )(xn, w_qkv, head_gain_cols, cos_t, sin_t)


def _rotary_tables(s):
    n_rows = s // GRID_W
    inv_freq = ROPE_THETA ** (-jnp.arange(ROPE_FREQS, dtype=F32) / ROPE_FREQS)
    ang_row = jnp.arange(n_rows, dtype=F32)[:, None] * inv_freq[None, :]
    ang_col = jnp.arange(GRID_W, dtype=F32)[:, None] * inv_freq[None, :]
    by_row = lambda t: jnp.repeat(t, GRID_W, axis=0)
    by_col = lambda t: jnp.tile(t, (n_rows, 1))
    cos_r, sin_r = by_row(jnp.cos(ang_row)), by_row(jnp.sin(ang_row))
    cos_c, sin_c = by_col(jnp.cos(ang_col)), by_col(jnp.sin(ang_col))
    cos_t = jnp.concatenate([cos_r, cos_r, cos_c, cos_c], axis=-1)
    sin_t = jnp.concatenate([-sin_r, sin_r, -sin_c, sin_c], axis=-1)
    return cos_t, sin_t


FLASH_CHUNKS_PER_STEP = 8


def _flash_kernel(q_ref, kt_ref, v_ref, o_ref, q_all, s_ref, m_ref, acc_ref):
    tq = q_ref.shape[0]
    n_chunks = kt_ref.shape[0]
    lane_tiles = KV_CHUNK // LANES

    for g in range(GROUP):
        q_all[g * tq:(g + 1) * tq, :] = q_ref[:, g * HEAD_DIM:(g + 1) * HEAD_DIM]

    def scores(c):
        return jnp.dot(q_all[...], kt_ref[c], preferred_element_type=F32)

    def absorb(c, s, m, acc):
        tiles = [s[:, t * LANES:(t + 1) * LANES] for t in range(lane_tiles)]
        tile_max = functools.reduce(jnp.maximum, tiles)
        m_new = jnp.maximum(m, jnp.max(tile_max, axis=-1, keepdims=True))
        alpha = jnp.exp2(m - m_new)
        p = jnp.concatenate([jnp.exp2(t - m_new) for t in tiles], axis=1).astype(BF16)
        start = pl.multiple_of(c * KV_CHUNK, KV_CHUNK)
        v_c = v_ref[pl.ds(start, KV_CHUNK), :]
        v_ones = jnp.concatenate([v_c, jnp.ones_like(v_c)], axis=1)
        pv = jnp.dot(p, v_ones, preferred_element_type=F32)
        return m_new, jnp.concatenate([alpha, alpha], axis=1) * acc + pv

    s_ref[...] = scores(0)
    m_ref[...] = jnp.full(m_ref.shape, -jnp.inf, F32)
    acc_ref[...] = jnp.zeros(acc_ref.shape, F32)

    def chunk_group(c0, is_last):
        state = (m_ref[...], acc_ref[...])
        s_cur = s_ref[...]
        for u in range(FLASH_CHUNKS_PER_STEP):
            final_chunk = is_last and u == FLASH_CHUNKS_PER_STEP - 1
            s_next = None if final_chunk else scores(c0 + u + 1)
            state = absorb(c0 + u, s_cur, *state)
            s_cur = s_next
        m_ref[...], acc_ref[...] = state
        if not is_last:
            s_ref[...] = s_cur

    n_groups = n_chunks // FLASH_CHUNKS_PER_STEP

    def group_body(it, _):
        chunk_group(it * FLASH_CHUNKS_PER_STEP, False)
        return 0

    lax.fori_loop(0, n_groups - 1, group_body, 0)
    chunk_group((n_groups - 1) * FLASH_CHUNKS_PER_STEP, True)

    for g in range(GROUP):
        r = slice(g * tq, (g + 1) * tq)
        o_ref[:, g * HEAD_DIM:(g + 1) * HEAD_DIM] = \
            (acc_ref[r, :HEAD_DIM] / acc_ref[r, HEAD_DIM:]).astype(o_ref.dtype)


def _flash_attention(q, kt, v, block_q=256):
    s = q.shape[0]
    n_chunks = s // KV_CHUNK
    assert n_chunks % FLASH_CHUNKS_PER_STEP == 0
    group_cols = GROUP * HEAD_DIM
    rows = GROUP * block_q
    stat = pltpu.VMEM((rows, LANES), F32)
    est = 4 * _nbytes((block_q, group_cols), BF16) + 4 * _nbytes((s, HEAD_DIM), BF16) \
        + _nbytes((rows, HEAD_DIM), BF16) + _nbytes((rows, KV_CHUNK), F32) \
        + 3 * _nbytes((rows, LANES), F32) + 8 * _nbytes((rows, KV_CHUNK), F32)
    return pl.pallas_call(
        _flash_kernel,
        grid=(N_KV_HEADS, s // block_q),
        in_specs=[pl.BlockSpec((block_q, group_cols), lambda h, i: (i, h)),
                  pl.BlockSpec((n_chunks, HEAD_DIM, KV_CHUNK), lambda h, i: (0, h, 0)),
                  pl.BlockSpec((s, HEAD_DIM), lambda h, i: (0, h))],
        out_specs=pl.BlockSpec((block_q, group_cols), lambda h, i: (i, h)),
        out_shape=jax.ShapeDtypeStruct((s, Q_DIM), BF16),
        scratch_shapes=[pltpu.VMEM((rows, HEAD_DIM), BF16),
                        pltpu.VMEM((rows, KV_CHUNK), F32), stat,
                        pltpu.VMEM((rows, 2 * HEAD_DIM), F32)],
        name="flash_attention",
        compiler_params=_compiler_params(("parallel", "parallel"), est),
    )(q, kt, v)


RGLRU_CHUNK = 512
SCAN_UNROLL = 8
SCAN_BLOCK = SCAN_UNROLL * SUBLANES
CONV_HALO = SUBLANES
N_SEGMENTS = SUBLANES


def _rglru_kernel(rec_ref, gate_ref, cw_ref, cb_ref, gw_ref, gb_ref, lam_ref, y_ref,
                  xpad_ref, af_ref, bf_ref, ab_ref, bb_ref):
    s_len = rec_ref.shape[0]
    seg_len = s_len // N_SEGMENTS
    n_chunks = s_len // RGLRU_CHUNK

    def scan_rows(start):
        seg = start // seg_len
        pos = start - seg * seg_len
        return seg, pl.ds(pos * N_SEGMENTS + seg, RGLRU_CHUNK, stride=N_SEGMENTS)

    zeros_halo = jnp.zeros((CONV_HALO, LANES), F32)
    xpad_ref[0:CONV_HALO, :] = zeros_halo
    xpad_ref[CONV_HALO + s_len:CONV_HALO + s_len + CONV_HALO, :] = zeros_halo
    xpad_ref[CONV_HALO:CONV_HALO + s_len, :] = rec_ref[...]

    neg_lam = -lam_ref[...]
    softplus = jnp.maximum(neg_lam, 0.0) + jnp.log(1.0 + jnp.exp(-jnp.abs(neg_lam)))
    half_decay = (-0.5 * C_DECAY * LOG2_E) * softplus

    def gates_body(c, _):
        start = pl.multiple_of(c * RGLRU_CHUNK, RGLRU_CHUNK)
        _, rows = scan_rows(start)
        xc = cb_ref[...]
        for tap in range(CONV_W):
            off = CONV_HALO - CONV_PAD_L + tap
            xc = xc + xpad_ref[pl.ds(start + off, RGLRU_CHUNK), :] * cw_ref[tap:tap + 1, :]
        t = jnp.tanh(jnp.dot(xc.astype(BF16), gw_ref[...], preferred_element_type=F32)
                     + gb_ref[...])
        xc_half = 0.5 * xc
        for direction, (a_ref, b_ref) in enumerate(((af_ref, bf_ref), (ab_ref, bb_ref))):
            t_r = t[:, (2 * direction) * LANES:(2 * direction + 1) * LANES]
            t_i = t[:, (2 * direction + 1) * LANES:(2 * direction + 2) * LANES]
            hd = half_decay[direction:direction + 1, :]
            a = jnp.exp2(hd * t_r + hd)
            a_ref[rows, :] = a
            b_ref[rows, :] = jnp.sqrt(1.0 - a * a) * (xc_half + xc_half * t_i)
        return 0

    lax.fori_loop(0, n_chunks, gates_body, 0)

    n_iters = s_len // SCAN_BLOCK

    def scan_block(a_ref, b_ref, start, h, prod, reverse):
        rows = pl.ds(start, SCAN_BLOCK)
        a_blk = a_ref[rows, :]
        b_blk = b_ref[rows, :]
        order = range(SCAN_UNROLL - 1, -1, -1) if reverse else range(SCAN_UNROLL)
        h_out = [None] * SCAN_UNROLL
        prod_out = [None] * SCAN_UNROLL
        for u in order:
            a8 = a_blk[u * SUBLANES:(u + 1) * SUBLANES]
            h = a8 * h + b_blk[u * SUBLANES:(u + 1) * SUBLANES]
            prod = a8 * prod
            h_out[u] = h
            prod_out[u] = prod
        b_ref[rows, :] = jnp.concatenate(h_out, axis=0)
        a_ref[rows, :] = jnp.concatenate(prod_out, axis=0)
        return h, prod

    def scan_body(it, carry):
        h_f, p_f, h_b, p_b = carry
        start_f = pl.multiple_of(it * SCAN_BLOCK, SCAN_BLOCK)
        start_b = pl.multiple_of((n_iters - 1 - it) * SCAN_BLOCK, SCAN_BLOCK)
        h_f, p_f = scan_block(af_ref, bf_ref, start_f, h_f, p_f, reverse=False)
        h_b, p_b = scan_block(ab_ref, bb_ref, start_b, h_b, p_b, reverse=True)
        return h_f, p_f, h_b, p_b

    zeros = jnp.zeros((N_SEGMENTS, LANES), F32)
    ones = jnp.ones((N_SEGMENTS, LANES), F32)
    end_f, prod_f, end_b, prod_b = lax.fori_loop(0, n_iters, scan_body, (zeros, ones, zeros, ones))

    def entering_states(end, prod, order):
        state = jnp.zeros((1, LANES), F32)
        entering = [None] * N_SEGMENTS
        for seg in order:
            entering[seg] = state
            state = prod[seg:seg + 1, :] * state + end[seg:seg + 1, :]
        return jnp.concatenate(entering, axis=0)

    enter_f = jnp.tile(entering_states(end_f, prod_f, range(N_SEGMENTS)), (SCAN_UNROLL, 1))
    enter_b = jnp.tile(entering_states(end_b, prod_b, range(N_SEGMENTS - 1, -1, -1)),
                       (SCAN_UNROLL, 1))

    def resolve_body(it, _):
        rows = pl.ds(pl.multiple_of(it * SCAN_BLOCK, SCAN_BLOCK), SCAN_BLOCK)
        bf_ref[rows, :] = (bf_ref[rows, :] + af_ref[rows, :] * enter_f) \
            + (bb_ref[rows, :] + ab_ref[rows, :] * enter_b)
        return 0

    lax.fori_loop(0, n_iters, resolve_body, 0)

    def out_body(c, _):
        start = pl.multiple_of(c * RGLRU_CHUNK, RGLRU_CHUNK)
        _, rows = scan_rows(start)
        y_ref[pl.ds(start, RGLRU_CHUNK), :] = \
            (gate_ref[pl.ds(start, RGLRU_CHUNK), :] * bf_ref[rows, :]).astype(y_ref.dtype)
        return 0

    lax.fori_loop(0, n_chunks, out_body, 0)


def _rglru(rec, gelu_gate, conv_w, conv_b, gate_w_half, gate_b_half, lam):
    s = rec.shape[0]
    col_spec = lambda rows: pl.BlockSpec((rows, LANES), lambda j: (0, j))
    seq_f32 = _nbytes((s, LANES), F32)
    est = 4 * seq_f32 + 2 * _nbytes((s, LANES), BF16) + 5 * seq_f32 \
        + 12 * _nbytes((RGLRU_CHUNK, 4 * LANES), F32)
    return pl.pallas_call(
        _rglru_kernel,
        grid=(N_RNN_BLOCKS,),
        in_specs=[col_spec(s), col_spec(s), col_spec(CONV_W), col_spec(1),
                  pl.BlockSpec((None, RNN_BLOCK, 4 * RNN_BLOCK), lambda j: (j, 0, 0)),
                  pl.BlockSpec((None, 1, 4 * RNN_BLOCK), lambda j: (j, 0, 0)),
                  col_spec(2)],
        out_specs=col_spec(s),
        out_shape=jax.ShapeDtypeStruct((s, D_RNN), BF16),
        scratch_shapes=[pltpu.VMEM((s + 2 * CONV_HALO, LANES), F32)]
        + [pltpu.VMEM((s, LANES), F32)] * 4,
        name="rglru",
        compiler_params=_compiler_params(("parallel",), est),
    )(rec, gelu_gate, conv_w, conv_b.reshape(1, D_RNN), gate_w_half, gate_b_half, lam)


def kernel(x, ffn_norm, ffn_w_gu, ffn_w_down, attn_norm, attn_w_qkv, attn_q_norm, attn_k_norm,
           attn_w_o, rec_norm, rec_w_in, rec_conv_w, rec_conv_b, rec_gate_w, rec_gate_b,
           rec_lambda, rec_w_out, final_norm):
    b, s, d = x.shape
    assert (b, d) == (1, D_MODEL) and s % 1024 == 0
    x = x.reshape(s, d)
    cos_t, sin_t = _rotary_tables(s)
    w_qkv = attn_w_qkv.astype(BF16)
    w_o = attn_w_o.astype(BF16)

    ffn_order = [(layer, half) for layer in range(DEPTH) for half in range(2)]
    w_gu_bf16 = {ffn_order[0]: ffn_w_gu[0, 0].astype(BF16)}

    def ffn(x, xn, layer, half, next_gain, norm_dtype):
        sides = [(ffn_w_down, (layer, half), 0)]
        position = ffn_order.index((layer, half))
        if position + 1 < len(ffn_order):
            sides.append((ffn_w_gu, ffn_order[position + 1], 1))
        h, w_down, *w_gu_next = _dual_matmul(
            xn, w_gu_bf16.pop((layer, half)), (), _swiglu_epilogue, (BF16,), "ffn_gate_up",
            sides=sides, block_rows=2048)
        if w_gu_next:
            w_gu_bf16[ffn_order[position + 1]] = w_gu_next[0]
        return _proj_residual_norm(h, w_down, (), x, next_gain, 0.5, norm_dtype, block_rows=256)

    xn = _rms_norm(x, ffn_norm[0, 0], BF16)
    for i in range(DEPTH):
        j = i // N_MIXERS
        is_attn = i % N_MIXERS == 0
        x, xn = ffn(x, xn, i, 0, attn_norm[j] if is_attn else rec_norm[j], BF16)
        if is_attn:
            q_scale = (HEAD_DIM ** -0.5) * LOG2_E
            head_gain_cols = jnp.concatenate(
                [jnp.tile(attn_q_norm[j] * q_scale, N_HEADS),
                 jnp.tile(attn_k_norm[j], N_KV_HEADS),
                 jnp.ones((KV_DIM,), F32)]).reshape(1, QKV_DIM)
            q, kt, v = _qkv_proj(xn, w_qkv, (j,), head_gain_cols, cos_t, sin_t)
            y = _flash_attention(q, kt, v)
            w_mix, lead = w_o, (j,)
        else:
            gelu_gate, rec, w_out = _dual_matmul(xn, rec_w_in, (j,), _gelu_split_epilogue,
                                                 (F32, F32), "rec_in_proj",
                                                 sides=[(rec_w_out, (j,), 0)])
            gate_w_half = (0.5 * jnp.transpose(rec_gate_w[j], (2, 3, 0, 1, 4))).reshape(
                N_RNN_BLOCKS, RNN_BLOCK, 4 * RNN_BLOCK).astype(BF16)
            gate_b_half = 0.5 * jnp.transpose(
                rec_gate_b[j].reshape(2, 2, N_RNN_BLOCKS, RNN_BLOCK), (2, 0, 1, 3)).reshape(
                N_RNN_BLOCKS, 1, 4 * RNN_BLOCK)
            y = _rglru(rec, gelu_gate, rec_conv_w[j], rec_conv_b[j], gate_w_half, gate_b_half,
                       rec_lambda[j])
            w_mix, lead = w_out, ()
        x, xn = _proj_residual_norm(y, w_mix, lead, x, ffn_norm[i, 1], 1.0, BF16, block_rows=512)
        last = i == DEPTH - 1
        next_gain = final_norm if last else ffn_norm[i + 1, 0]
        x, xn = ffn(x, xn, i, 1, next_gain, F32 if last else BF16)
    return xn.reshape(b, s, d)
```

```python
import functools
import math

import jax
import jax.numpy as jnp
import numpy as np
from jax import lax
from jax.experimental import pallas as pl
from jax.experimental.pallas import tpu as pltpu

D_MODEL = 2048
DEPTH = 4
N_MIXERS = 2
GRID_W = 64
HEAD_DIM = 128
N_HEADS = D_MODEL // HEAD_DIM
N_KV_HEADS = 4
GROUP = N_HEADS // N_KV_HEADS
Q_DIM = N_HEADS * HEAD_DIM
KV_DIM = N_KV_HEADS * HEAD_DIM
QKV_DIM = Q_DIM + 2 * KV_DIM
ROPE_THETA = 10000.0
ROPE_FREQS = HEAD_DIM // 4
D_RNN = D_MODEL
RNN_BLOCK = 128
N_RNN_BLOCKS = D_RNN // RNN_BLOCK
CONV_W = 4
CONV_PAD_L = 2
C_DECAY = 8.0
D_FF = 5632
NORM_EPS = 1e-6

V7X_VMEM_USABLE_BYTES = 58 * 1024 * 1024
SUBLANES = 8
LANES = 128

F32 = jnp.float32
BF16 = jnp.bfloat16
LOG2_E = math.log2(math.e)


def _nbytes(shape, dtype):
    return int(np.prod(shape)) * jnp.dtype(dtype).itemsize


def _compiler_params(semantics, vmem_estimate_bytes):
    limit = min(V7X_VMEM_USABLE_BYTES, max(32 * 1024 * 1024, int(vmem_estimate_bytes)))
    return pltpu.CompilerParams(dimension_semantics=semantics, vmem_limit_bytes=limit)


def _rms_scale(x):
    return lax.rsqrt(jnp.mean(x * x, axis=-1, keepdims=True) + NORM_EPS)


def _stacked_spec(block_shape, lead, index_map):
    return pl.BlockSpec((None,) * len(lead) + tuple(block_shape),
                        lambda *g: tuple(lead) + tuple(index_map(*g)))


def _rms_norm_kernel(x_ref, g_ref, o_ref):
    x = x_ref[...]
    o_ref[...] = (x * _rms_scale(x) * g_ref[...]).astype(o_ref.dtype)


def _rms_norm(x, gain, out_dtype, block_rows=512):
    s, d = x.shape
    est = 2 * (_nbytes((block_rows, d), F32) + _nbytes((block_rows, d), out_dtype)) \
        + 2 * _nbytes((block_rows, d), F32)
    return pl.pallas_call(
        _rms_norm_kernel,
        grid=(s // block_rows,),
        in_specs=[pl.BlockSpec((block_rows, d), lambda i: (i, 0)),
                  pl.BlockSpec((1, d), lambda i: (0, 0))],
        out_specs=pl.BlockSpec((block_rows, d), lambda i: (i, 0)),
        out_shape=jax.ShapeDtypeStruct((s, d), out_dtype),
        name="rms_norm",
        compiler_params=_compiler_params(("parallel",), est),
    )(x, gain.reshape(1, d))


def _swiglu_epilogue(a, b):
    return ((a / (1.0 + jnp.exp(-a))) * b,)


def _gelu_split_epilogue(a, b):
    c = math.sqrt(2.0 / math.pi)
    cdf = 0.5 * (1.0 + jnp.tanh(c * (a + 0.044715 * (a * a * a))))
    return (a * cdf, b)


DUAL_MATMUL_SUB_ROWS = 512


def _dual_matmul_kernel(x_ref, wa_ref, wb_ref, *refs, epilogue, n_sides):
    side_refs = refs[:n_sides]
    out_refs = refs[n_sides:len(refs) - n_sides]
    side_out_refs = refs[len(refs) - n_sides:]
    for side_ref, side_out_ref in zip(side_refs, side_out_refs):
        side_out_ref[...] = side_ref[...].astype(side_out_ref.dtype)
    wa = wa_ref[...].astype(BF16)
    wb = wb_ref[...].astype(BF16)
    for r0 in range(0, x_ref.shape[0], DUAL_MATMUL_SUB_ROWS):
        rows = slice(r0, r0 + DUAL_MATMUL_SUB_ROWS)
        x = x_ref[rows, :]
        a = jnp.dot(x, wa, preferred_element_type=F32)
        b = jnp.dot(x, wb, preferred_element_type=F32)
        for ref, val in zip(out_refs, epilogue(a, b)):
            ref[rows, :] = val.astype(ref.dtype)


def _dual_matmul(x, w, lead, epilogue, out_dtypes, name, sides=(), block_rows=1024,
                 block_cols=512):
    s, k = x.shape
    n = w.shape[-1] // 2
    n_col_blocks = n // block_cols
    n_row_blocks = s // block_rows
    est = 2 * _nbytes((block_rows, k), BF16) + 4 * _nbytes((k, block_cols), w.dtype) \
        + 2 * _nbytes((k, block_cols), BF16) \
        + sum(2 * _nbytes((block_rows, block_cols), dt) for dt in out_dtypes) \
        + 4 * _nbytes((block_rows, block_cols), F32)
    out_spec = pl.BlockSpec((block_rows, block_cols), lambda i, j: (i, j))
    in_specs = [pl.BlockSpec((block_rows, k), lambda i, j: (i, 0)),
                _stacked_spec((k, block_cols), lead, lambda i, j: (0, j)),
                _stacked_spec((k, block_cols), lead, lambda i, j: (0, j + n_col_blocks))]
    out_specs = [out_spec] * len(out_dtypes)
    out_shape = [jax.ShapeDtypeStruct((s, n), dt) for dt in out_dtypes]
    operands = [x, w, w]
    slab_index = lambda i, j: (i * n_col_blocks + j, 0)
    for side_w, side_lead in sides:
        side_rows, side_cols = side_w.shape[-2:]
        slab, rem = divmod(side_rows, n_row_blocks * n_col_blocks)
        assert rem == 0 and slab % (2 * SUBLANES) == 0
        in_specs.append(_stacked_spec((slab, side_cols), side_lead, slab_index))
        out_specs.append(pl.BlockSpec((slab, side_cols), slab_index))
        out_shape.append(jax.ShapeDtypeStruct((side_rows, side_cols), BF16))
        operands.append(side_w)
        est += 2 * _nbytes((slab, side_cols), F32) + 2 * _nbytes((slab, side_cols), BF16)
    return pl.pallas_call(
        functools.partial(_dual_matmul_kernel, epilogue=epilogue, n_sides=len(sides)),
        grid=(n_row_blocks, n_col_blocks),
        in_specs=in_specs,
        out_specs=out_specs,
        out_shape=out_shape,
        name=name,
        compiler_params=_compiler_params(("parallel", "arbitrary"), est),
    )(*operands)


def _proj_residual_norm_kernel(y_ref, w_ref, x_ref, g_ref, xo_ref, xn_ref, *, branch_scale):
    acc = jnp.dot(y_ref[...], w_ref[...], preferred_element_type=F32)
    x_new = x_ref[...] + branch_scale * acc
    xo_ref[...] = x_new
    xn_ref[...] = (x_new * _rms_scale(x_new) * g_ref[...]).astype(xn_ref.dtype)


def _proj_residual_norm(y, w, lead, x, next_gain, branch_scale, norm_dtype, block_rows):
    s, k = y.shape
    d = w.shape[-1]
    est = _nbytes((k, d), BF16) + 2 * _nbytes((block_rows, k), BF16) \
        + 4 * _nbytes((block_rows, d), F32) + 2 * _nbytes((block_rows, d), norm_dtype) \
        + 3 * _nbytes((block_rows, d), F32)
    row_spec = lambda cols: pl.BlockSpec((block_rows, cols), lambda i: (i, 0))
    w_spec = pl.BlockSpec((None,) * len(lead) + (k, d), lambda i: tuple(lead) + (0, 0),
                          pipeline_mode=pl.Buffered(1))
    return pl.pallas_call(
        functools.partial(_proj_residual_norm_kernel, branch_scale=branch_scale),
        grid=(s // block_rows,),
        in_specs=[row_spec(k), w_spec, row_spec(d), pl.BlockSpec((1, d), lambda i: (0, 0))],
        out_specs=[row_spec(d), row_spec(d)],
        out_shape=[jax.ShapeDtypeStruct((s, d), F32), jax.ShapeDtypeStruct((s, d), norm_dtype)],
        name=f"proj_residual_norm_k{k}",
        compiler_params=_compiler_params(("parallel",), est),
    )(y, w, x, next_gain.reshape(1, d))


KV_CHUNK = 512
QKV_COLS_PER_DOT = 512


def _swap_rotary_halves(x):
    lane = lax.broadcasted_iota(jnp.int32, x.shape, 1)
    first_quarter = (lane % (2 * ROPE_FREQS)) < ROPE_FREQS
    return jnp.where(first_quarter,
                     pltpu.roll(x, HEAD_DIM - ROPE_FREQS, 1),
                     pltpu.roll(x, ROPE_FREQS, 1))


def _qkv_kernel(x_ref, w_ref, gain_ref, cos_ref, sin_ref, q_ref, kt_ref, v_ref):
    x = x_ref[...]
    cos_t = cos_ref[...]
    sin_t = sin_ref[...]
    heads_per_dot = QKV_COLS_PER_DOT // HEAD_DIM

    def normed_rotated(acc, h, col0):
        cols = slice(col0 + h * HEAD_DIM, col0 + (h + 1) * HEAD_DIM)
        xh = acc[:, h * HEAD_DIM:(h + 1) * HEAD_DIM]
        xh = xh * _rms_scale(xh) * gain_ref[:, cols]
        return xh * cos_t + _swap_rotary_halves(xh) * sin_t

    for jb in range(QKV_DIM // QKV_COLS_PER_DOT):
        col0 = jb * QKV_COLS_PER_DOT
        acc = jnp.dot(x, w_ref[:, col0:col0 + QKV_COLS_PER_DOT], preferred_element_type=F32)
        if col0 < Q_DIM:
            for h in range(heads_per_dot):
                q_ref[:, col0 + h * HEAD_DIM:col0 + (h + 1) * HEAD_DIM] = \
                    normed_rotated(acc, h, col0).astype(q_ref.dtype)
        elif col0 < Q_DIM + KV_DIM:
            for h in range(heads_per_dot):
                row0 = col0 - Q_DIM + h * HEAD_DIM
                kt_ref[row0:row0 + HEAD_DIM, :] = \
                    normed_rotated(acc, h, col0).T.astype(kt_ref.dtype)
        else:
            v0 = col0 - Q_DIM - KV_DIM
            v_ref[:, v0:v0 + QKV_COLS_PER_DOT] = acc.astype(v_ref.dtype)


def _qkv_proj(xn, w_qkv, lead, head_gain_cols, cos_t, sin_t):
    s, k = xn.shape
    rows = KV_CHUNK
    est = _nbytes((k, QKV_DIM), BF16) + 2 * _nbytes((rows, k), BF16) \
        + 4 * _nbytes((rows, HEAD_DIM), F32) + 2 * _nbytes((rows, QKV_DIM), BF16) \
        + 6 * _nbytes((rows, QKV_COLS_PER_DOT), F32)
    return pl.pallas_call(
        _qkv_kernel,
        grid=(s // rows,),
        in_specs=[pl.BlockSpec((rows, k), lambda i: (i, 0)),
                  pl.BlockSpec((None,) * len(lead) + (k, QKV_DIM),
                               lambda i: tuple(lead) + (0, 0), pipeline_mode=pl.Buffered(1)),
                  pl.BlockSpec((1, QKV_DIM), lambda i: (0, 0)),
                  pl.BlockSpec((rows, HEAD_DIM), lambda i: (i, 0)),
                  pl.BlockSpec((rows, HEAD_DIM), lambda i: (i, 0))],
        out_specs=[pl.BlockSpec((rows, Q_DIM), lambda i: (i, 0)),
                   pl.BlockSpec((None, KV_DIM, rows), lambda i: (i, 0, 0)),
                   pl.BlockSpec((rows, KV_DIM), lambda i: (i, 0))],
        out_shape=[jax.ShapeDtypeStruct((s, Q_DIM), BF16),
                   jax.ShapeDtypeStruct((s // rows, KV_DIM, rows), BF16),
                   jax.ShapeDtypeStruct((s, KV_DIM), BF16)],
        name="qkv_proj",
        compiler_params=_compiler_params(("parallel",), est),
    )(xn, w_qkv, head_gain_cols, cos_t, sin_t)


def _rotary_tables(s):
    n_rows = s // GRID_W
    inv_freq = ROPE_THETA ** (-jnp.arange(ROPE_FREQS, dtype=F32) / ROPE_FREQS)
    ang_row = jnp.arange(n_rows, dtype=F32)[:, None] * inv_freq[None, :]
    ang_col = jnp.arange(GRID_W, dtype=F32)[:, None] * inv_freq[None, :]
    by_row = lambda t: jnp.repeat(t, GRID_W, axis=0)
    by_col = lambda t: jnp.tile(t, (n_rows, 1))
    cos_r, sin_r = by_row(jnp.cos(ang_row)), by_row(jnp.sin(ang_row))
    cos_c, sin_c = by_col(jnp.cos(ang_col)), by_col(jnp.sin(ang_col))
    cos_t = jnp.concatenate([cos_r, cos_r, cos_c, cos_c], axis=-1)
    sin_t = jnp.concatenate([-sin_r, sin_r, -sin_c, sin_c], axis=-1)
    return cos_t, sin_t


FLASH_CHUNKS_PER_STEP = 8


def _flash_kernel(q_ref, kt_ref, v_ref, o_ref, q_all, s_ref, m_ref, acc_ref):
    tq = q_ref.shape[0]
    n_chunks = kt_ref.shape[0]
    lane_tiles = KV_CHUNK // LANES

    for g in range(GROUP):
        q_all[g * tq:(g + 1) * tq, :] = q_ref[:, g * HEAD_DIM:(g + 1) * HEAD_DIM]

    def scores(c):
        return jnp.dot(q_all[...], kt_ref[c], preferred_element_type=F32)

    def absorb(c, s, m, acc):
        tiles = [s[:, t * LANES:(t + 1) * LANES] for t in range(lane_tiles)]
        tile_max = functools.reduce(jnp.maximum, tiles)
        m_new = jnp.maximum(m, jnp.max(tile_max, axis=-1, keepdims=True))
        alpha = jnp.exp2(m - m_new)
        p = jnp.concatenate([jnp.exp2(t - m_new) for t in tiles], axis=1).astype(BF16)
        start = pl.multiple_of(c * KV_CHUNK, KV_CHUNK)
        v_c = v_ref[pl.ds(start, KV_CHUNK), :]
        v_ones = jnp.concatenate([v_c, jnp.ones_like(v_c)], axis=1)
        pv = jnp.dot(p, v_ones, preferred_element_type=F32)
        return m_new, jnp.concatenate([alpha, alpha], axis=1) * acc + pv

    s_ref[...] = scores(0)
    m_ref[...] = jnp.full(m_ref.shape, -jnp.inf, F32)
    acc_ref[...] = jnp.zeros(acc_ref.shape, F32)

    def chunk_group(c0, is_last):
        state = (m_ref[...], acc_ref[...])
        s_cur = s_ref[...]
        for u in range(FLASH_CHUNKS_PER_STEP):
            final_chunk = is_last and u == FLASH_CHUNKS_PER_STEP - 1
            s_next = None if final_chunk else scores(c0 + u + 1)
            state = absorb(c0 + u, s_cur, *state)
            s_cur = s_next
        m_ref[...], acc_ref[...] = state
        if not is_last:
            s_ref[...] = s_cur

    n_groups = n_chunks // FLASH_CHUNKS_PER_STEP

    def group_body(it, _):
        chunk_group(it * FLASH_CHUNKS_PER_STEP, False)
        return 0

    lax.fori_loop(0, n_groups - 1, group_body, 0)
    chunk_group((n_groups - 1) * FLASH_CHUNKS_PER_STEP, True)

    for g in range(GROUP):
        r = slice(g * tq, (g + 1) * tq)
        o_ref[:, g * HEAD_DIM:(g + 1) * HEAD_DIM] = \
            (acc_ref[r, :HEAD_DIM] / acc_ref[r, HEAD_DIM:]).astype(o_ref.dtype)


def _flash_attention(q, kt, v, block_q=256):
    s = q.shape[0]
    n_chunks = s // KV_CHUNK
    assert n_chunks % FLASH_CHUNKS_PER_STEP == 0
    group_cols = GROUP * HEAD_DIM
    rows = GROUP * block_q
    stat = pltpu.VMEM((rows, LANES), F32)
    est = 4 * _nbytes((block_q, group_cols), BF16) + 4 * _nbytes((s, HEAD_DIM), BF16) \
        + _nbytes((rows, HEAD_DIM), BF16) + _nbytes((rows, KV_CHUNK), F32) \
        + 3 * _nbytes((rows, LANES), F32) + 8 * _nbytes((rows, KV_CHUNK), F32)
    return pl.pallas_call(
        _flash_kernel,
        grid=(N_KV_HEADS, s // block_q),
        in_specs=[pl.BlockSpec((block_q, group_cols), lambda h, i: (i, h)),
                  pl.BlockSpec((n_chunks, HEAD_DIM, KV_CHUNK), lambda h, i: (0, h, 0)),
                  pl.BlockSpec((s, HEAD_DIM), lambda h, i: (0, h))],
        out_specs=pl.BlockSpec((block_q, group_cols), lambda h, i: (i, h)),
        out_shape=jax.ShapeDtypeStruct((s, Q_DIM), BF16),
        scratch_shapes=[pltpu.VMEM((rows, HEAD_DIM), BF16),
                        pltpu.VMEM((rows, KV_CHUNK), F32), stat,
                        pltpu.VMEM((rows, 2 * HEAD_DIM), F32)],
        name="flash_attention",
        compiler_params=_compiler_params(("parallel", "parallel"), est),
    )(q, kt, v)


RGLRU_CHUNK = 512
SCAN_UNROLL = 8
SCAN_BLOCK = SCAN_UNROLL * SUBLANES
CONV_HALO = SUBLANES
N_SEGMENTS = SUBLANES


def _rglru_kernel(rec_ref, gate_ref, cw_ref, cb_ref, gw_ref, gb_ref, lam_ref, y_ref,
                  xpad_ref, af_ref, bf_ref, ab_ref, bb_ref):
    s_len = rec_ref.shape[0]
    seg_len = s_len // N_SEGMENTS
    n_chunks = s_len // RGLRU_CHUNK

    def scan_rows(start):
        seg = start // seg_len
        pos = start - seg * seg_len
        return seg, pl.ds(pos * N_SEGMENTS + seg, RGLRU_CHUNK, stride=N_SEGMENTS)

    zeros_halo = jnp.zeros((CONV_HALO, LANES), F32)
    xpad_ref[0:CONV_HALO, :] = zeros_halo
    xpad_ref[CONV_HALO + s_len:CONV_HALO + s_len + CONV_HALO, :] = zeros_halo
    xpad_ref[CONV_HALO:CONV_HALO + s_len, :] = rec_ref[...]

    neg_lam = -lam_ref[...]
    softplus = jnp.maximum(neg_lam, 0.0) + jnp.log(1.0 + jnp.exp(-jnp.abs(neg_lam)))
    half_decay = (-0.5 * C_DECAY * LOG2_E) * softplus

    def gates_body(c, _):
        start = pl.multiple_of(c * RGLRU_CHUNK, RGLRU_CHUNK)
        _, rows = scan_rows(start)
        xc = cb_ref[...]
        for tap in range(CONV_W):
            off = CONV_HALO - CONV_PAD_L + tap
            xc = xc + xpad_ref[pl.ds(start + off, RGLRU_CHUNK), :] * cw_ref[tap:tap + 1, :]
        t = jnp.tanh(jnp.dot(xc.astype(BF16), gw_ref[...], preferred_element_type=F32)
                     + gb_ref[...])
        xc_half = 0.5 * xc
        for direction, (a_ref, b_ref) in enumerate(((af_ref, bf_ref), (ab_ref, bb_ref))):
            t_r = t[:, (2 * direction) * LANES:(2 * direction + 1) * LANES]
            t_i = t[:, (2 * direction + 1) * LANES:(2 * direction + 2) * LANES]
            hd = half_decay[direction:direction + 1, :]
            a = jnp.exp2(hd * t_r + hd)
            a_ref[rows, :] = a
            b_ref[rows, :] = jnp.sqrt(1.0 - a * a) * (xc_half + xc_half * t_i)
        return 0

    lax.fori_loop(0, n_chunks, gates_body, 0)

    n_iters = s_len // SCAN_BLOCK

    def scan_block(a_ref, b_ref, start, h, prod, reverse):
        rows = pl.ds(start, SCAN_BLOCK)
        a_blk = a_ref[rows, :]
        b_blk = b_ref[rows, :]
        order = range(SCAN_UNROLL - 1, -1, -1) if reverse else range(SCAN_UNROLL)
        h_out = [None] * SCAN_UNROLL
        prod_out = [None] * SCAN_UNROLL
        for u in order:
            a8 = a_blk[u * SUBLANES:(u + 1) * SUBLANES]
            h = a8 * h + b_blk[u * SUBLANES:(u + 1) * SUBLANES]
            prod = a8 * prod
            h_out[u] = h
            prod_out[u] = prod
        b_ref[rows, :] = jnp.concatenate(h_out, axis=0)
        a_ref[rows, :] = jnp.concatenate(prod_out, axis=0)
        return h, prod

    def scan_body(it, carry):
        h_f, p_f, h_b, p_b = carry
        start_f = pl.multiple_of(it * SCAN_BLOCK, SCAN_BLOCK)
        start_b = pl.multiple_of((n_iters - 1 - it) * SCAN_BLOCK, SCAN_BLOCK)
        h_f, p_f = scan_block(af_ref, bf_ref, start_f, h_f, p_f, reverse=False)
        h_b, p_b = scan_block(ab_ref, bb_ref, start_b, h_b, p_b, reverse=True)
        return h_f, p_f, h_b, p_b

    zeros = jnp.zeros((N_SEGMENTS, LANES), F32)
    ones = jnp.ones((N_SEGMENTS, LANES), F32)
    end_f, prod_f, end_b, prod_b = lax.fori_loop(0, n_iters, scan_body, (zeros, ones, zeros, ones))

    def entering_states(end, prod, order):
        state = jnp.zeros((1, LANES), F32)
        entering = [None] * N_SEGMENTS
        for seg in order:
            entering[seg] = state
            state = prod[seg:seg + 1, :] * state + end[seg:seg + 1, :]
        return jnp.concatenate(entering, axis=0)

    enter_f = jnp.tile(entering_states(end_f, prod_f, range(N_SEGMENTS)), (SCAN_UNROLL, 1))
    enter_b = jnp.tile(entering_states(end_b, prod_b, range(N_SEGMENTS - 1, -1, -1)),
                       (SCAN_UNROLL, 1))

    def resolve_body(it, _):
        rows = pl.ds(pl.multiple_of(it * SCAN_BLOCK, SCAN_BLOCK), SCAN_BLOCK)
        bf_ref[rows, :] = (bf_ref[rows, :] + af_ref[rows, :] * enter_f) \
            + (bb_ref[rows, :] + ab_ref[rows, :] * enter_b)
        return 0

    lax.fori_loop(0, n_iters, resolve_body, 0)

    def out_body(c, _):
        start = pl.multiple_of(c * RGLRU_CHUNK, RGLRU_CHUNK)
        _, rows = scan_rows(start)
        y_ref[pl.ds(start, RGLRU_CHUNK), :] = \
            (gate_ref[pl.ds(start, RGLRU_CHUNK), :] * bf_ref[rows, :]).astype(y_ref.dtype)
        return 0

    lax.fori_loop(0, n_chunks, out_body, 0)


def _rglru(rec, gelu_gate, conv_w, conv_b, gate_w_half, gate_b_half, lam):
    s = rec.shape[0]
    col_spec = lambda rows: pl.BlockSpec((rows, LANES), lambda j: (0, j))
    seq_f32 = _nbytes((s, LANES), F32)
    est = 4 * seq_f32 + 2 * _nbytes((s, LANES), BF16) + 5 * seq_f32 \
        + 12 * _nbytes((RGLRU_CHUNK, 4 * LANES), F32)
    return pl.pallas_call(
        _rglru_kernel,
        grid=(N_RNN_BLOCKS,),
        in_specs=[col_spec(s), col_spec(s), col_spec(CONV_W), col_spec(1),
                  pl.BlockSpec((None, RNN_BLOCK, 4 * RNN_BLOCK), lambda j: (j, 0, 0)),
                  pl.BlockSpec((None, 1, 4 * RNN_BLOCK), lambda j: (j, 0, 0)),
                  col_spec(2)],
        out_specs=col_spec(s),
        out_shape=jax.ShapeDtypeStruct((s, D_RNN), BF16),
        scratch_shapes=[pltpu.VMEM((s + 2 * CONV_HALO, LANES), F32)]
        + [pltpu.VMEM((s, LANES), F32)] * 4,
        name="rglru",
        compiler_params=_compiler_params(("parallel",), est),
    )(rec, gelu_gate, conv_w, conv_b.reshape(1, D_RNN), gate_w_half, gate_b_half, lam)


def kernel(x, ffn_norm, ffn_w_gu, ffn_w_down, attn_norm, attn_w_qkv, attn_q_norm, attn_k_norm,
           attn_w_o, rec_norm, rec_w_in, rec_conv_w, rec_conv_b, rec_gate_w, rec_gate_b,
           rec_lambda, rec_w_out, final_norm):
    b, s, d = x.shape
    assert (b, d) == (1, D_MODEL) and s % 1024 == 0
    x = x.reshape(s, d)
    cos_t, sin_t = _rotary_tables(s)
    w_qkv = attn_w_qkv.astype(BF16)
    w_o = attn_w_o.astype(BF16)

    def ffn(x, xn, layer, half, next_gain, norm_dtype):
        h, w_down = _dual_matmul(xn, ffn_w_gu, (layer, half), _swiglu_epilogue, (BF16,),
                                 "ffn_gate_up", sides=[(ffn_w_down, (layer, half))],
                                 block_rows=2048)
        return _proj_residual_norm(h, w_down, (), x, next_gain, 0.5, norm_dtype, block_rows=256)

    xn = _rms_norm(x, ffn_norm[0, 0], BF16)
    for i in range(DEPTH):
        j = i // N_MIXERS
        is_attn = i % N_MIXERS == 0
        x, xn = ffn(x, xn, i, 0, attn_norm[j] if is_attn else rec_norm[j], BF16)
        if is_attn:
            q_scale = (HEAD_DIM ** -0.5) * LOG2_E
            head_gain_cols = jnp.concatenate(
                [jnp.tile(attn_q_norm[j] * q_scale, N_HEADS),
                 jnp.tile(attn_k_norm[j], N_KV_HEADS),
                 jnp.ones((KV_DIM,), F32)]).reshape(1, QKV_DIM)
            q, kt, v = _qkv_proj(xn, w_qkv, (j,), head_gain_cols, cos_t, sin_t)
            y = _flash_attention(q, kt, v)
            w_mix, lead = w_o, (j,)
        else:
            gelu_gate, rec, w_out = _dual_matmul(xn, rec_w_in, (j,), _gelu_split_epilogue,
                                                 (F32, F32), "rec_in_proj",
                                                 sides=[(rec_w_out, (j,))])
            gate_w_half = (0.5 * jnp.transpose(rec_gate_w[j], (2, 3, 0, 1, 4))).reshape(
                N_RNN_BLOCKS, RNN_BLOCK, 4 * RNN_BLOCK).astype(BF16)
            gate_b_half = 0.5 * jnp.transpose(
                rec_gate_b[j].reshape(2, 2, N_RNN_BLOCKS, RNN_BLOCK), (2, 0, 1, 3)).reshape(
                N_RNN_BLOCKS, 1, 4 * RNN_BLOCK)
            y = _rglru(rec, gelu_gate, rec_conv_w[j], rec_conv_b[j], gate_w_half, gate_b_half,
                       rec_lambda[j])
            w_mix, lead = w_out, ()
        x, xn = _proj_residual_norm(y, w_mix, lead, x, ffn_norm[i, 1], 1.0, BF16, block_rows=512)
        last = i == DEPTH - 1
        next_gain = final_norm if last else ffn_norm[i + 1, 0]
        x, xn = ffn(x, xn, i, 1, next_gain, F32 if last else BF16)
    return xn.reshape(b, s, d)
```

```python
import functools
import math

import jax
import jax.numpy as jnp
import numpy as np
from jax import lax
from jax.experimental import pallas as pl
from jax.experimental.pallas import tpu as pltpu

D_MODEL = 2048
DEPTH = 4
N_MIXERS = 2
GRID_W = 64
HEAD_DIM = 128
N_HEADS = D_MODEL // HEAD_DIM
N_KV_HEADS = 4
GROUP = N_HEADS // N_KV_HEADS
Q_DIM = N_HEADS * HEAD_DIM
KV_DIM = N_KV_HEADS * HEAD_DIM
QKV_DIM = Q_DIM + 2 * KV_DIM
ROPE_THETA = 10000.0
ROPE_FREQS = HEAD_DIM // 4
D_RNN = D_MODEL
RNN_BLOCK = 128
N_RNN_BLOCKS = D_RNN // RNN_BLOCK
CONV_W = 4
CONV_PAD_L = 2
C_DECAY = 8.0
D_FF = 5632
NORM_EPS = 1e-6

V7X_VMEM_USABLE_BYTES = 58 * 1024 * 1024
SUBLANES = 8
LANES = 128

F32 = jnp.float32
BF16 = jnp.bfloat16
LOG2_E = math.log2(math.e)


def _nbytes(shape, dtype):
    return int(np.prod(shape)) * jnp.dtype(dtype).itemsize


def _compiler_params(semantics, vmem_estimate_bytes):
    limit = min(V7X_VMEM_USABLE_BYTES, max(32 * 1024 * 1024, int(vmem_estimate_bytes)))
    return pltpu.CompilerParams(dimension_semantics=semantics, vmem_limit_bytes=limit)


def _rms_scale(x):
    return lax.rsqrt(jnp.mean(x * x, axis=-1, keepdims=True) + NORM_EPS)


def _stacked_spec(block_shape, lead, index_map):
    return pl.BlockSpec((None,) * len(lead) + tuple(block_shape),
                        lambda *g: tuple(lead) + tuple(index_map(*g)))


def _rms_norm_kernel(x_ref, g_ref, o_ref):
    x = x_ref[...]
    o_ref[...] = (x * _rms_scale(x) * g_ref[...]).astype(o_ref.dtype)


def _rms_norm(x, gain, out_dtype, block_rows=512):
    s, d = x.shape
    est = 2 * (_nbytes((block_rows, d), F32) + _nbytes((block_rows, d), out_dtype)) \
        + 2 * _nbytes((block_rows, d), F32)
    return pl.pallas_call(
        _rms_norm_kernel,
        grid=(s // block_rows,),
        in_specs=[pl.BlockSpec((block_rows, d), lambda i: (i, 0)),
                  pl.BlockSpec((1, d), lambda i: (0, 0))],
        out_specs=pl.BlockSpec((block_rows, d), lambda i: (i, 0)),
        out_shape=jax.ShapeDtypeStruct((s, d), out_dtype),
        name="rms_norm",
        compiler_params=_compiler_params(("parallel",), est),
    )(x, gain.reshape(1, d))


def _swiglu_epilogue(a, b):
    return ((a / (1.0 + jnp.exp(-a))) * b,)


def _gelu_split_epilogue(a, b):
    c = math.sqrt(2.0 / math.pi)
    cdf = 0.5 * (1.0 + jnp.tanh(c * (a + 0.044715 * (a * a * a))))
    return (a * cdf, b)


DUAL_MATMUL_SUB_ROWS = 512


def _dual_matmul_kernel(x_ref, wa_ref, wb_ref, *refs, epilogue, n_sides):
    side_refs = refs[:n_sides]
    out_refs = refs[n_sides:len(refs) - n_sides]
    side_out_refs = refs[len(refs) - n_sides:]
    for side_ref, side_out_ref in zip(side_refs, side_out_refs):
        side_out_ref[...] = side_ref[...].astype(side_out_ref.dtype)
    wa = wa_ref[...].astype(BF16)
    wb = wb_ref[...].astype(BF16)
    for r0 in range(0, x_ref.shape[0], DUAL_MATMUL_SUB_ROWS):
        rows = slice(r0, r0 + DUAL_MATMUL_SUB_ROWS)
        x = x_ref[rows, :]
        a = jnp.dot(x, wa, preferred_element_type=F32)
        b = jnp.dot(x, wb, preferred_element_type=F32)
        for ref, val in zip(out_refs, epilogue(a, b)):
            ref[rows, :] = val.astype(ref.dtype)


def _dual_matmul(x, w, lead, epilogue, out_dtypes, name, sides=(), block_rows=1024,
                 block_cols=512):
    s, k = x.shape
    n = w.shape[-1] // 2
    n_col_blocks = n // block_cols
    n_row_blocks = s // block_rows
    est = 2 * _nbytes((block_rows, k), BF16) + 4 * _nbytes((k, block_cols), w.dtype) \
        + 2 * _nbytes((k, block_cols), BF16) \
        + sum(2 * _nbytes((block_rows, block_cols), dt) for dt in out_dtypes) \
        + 4 * _nbytes((block_rows, block_cols), F32)
    out_spec = pl.BlockSpec((block_rows, block_cols), lambda i, j: (i, j))
    in_specs = [pl.BlockSpec((block_rows, k), lambda i, j: (i, 0)),
                _stacked_spec((k, block_cols), lead, lambda i, j: (0, j)),
                _stacked_spec((k, block_cols), lead, lambda i, j: (0, j + n_col_blocks))]
    out_specs = [out_spec] * len(out_dtypes)
    out_shape = [jax.ShapeDtypeStruct((s, n), dt) for dt in out_dtypes]
    operands = [x, w, w]
    slab_index = lambda i, j: (i * n_col_blocks + j, 0)
    for side_w, side_lead in sides:
        side_rows, side_cols = side_w.shape[-2:]
        slab, rem = divmod(side_rows, n_row_blocks * n_col_blocks)
        assert rem == 0 and slab % (2 * SUBLANES) == 0
        in_specs.append(_stacked_spec((slab, side_cols), side_lead, slab_index))
        out_specs.append(pl.BlockSpec((slab, side_cols), slab_index))
        out_shape.append(jax.ShapeDtypeStruct((side_rows, side_cols), BF16))
        operands.append(side_w)
        est += 2 * _nbytes((slab, side_cols), F32) + 2 * _nbytes((slab, side_cols), BF16)
    return pl.pallas_call(
        functools.partial(_dual_matmul_kernel, epilogue=epilogue, n_sides=len(sides)),
        grid=(n_row_blocks, n_col_blocks),
        in_specs=in_specs,
        out_specs=out_specs,
        out_shape=out_shape,
        name=name,
        compiler_params=_compiler_params(("parallel", "arbitrary"), est),
    )(*operands)


PROJ_SUB_ROWS = 256


def _proj_residual_norm_kernel(y_ref, w_ref, x_ref, g_ref, xo_ref, xn_ref, *, branch_scale):
    gain = g_ref[...]
    for r0 in range(0, y_ref.shape[0], PROJ_SUB_ROWS):
        rows = slice(r0, r0 + PROJ_SUB_ROWS)
        acc = jnp.dot(y_ref[rows, :], w_ref[...], preferred_element_type=F32)
        x_new = x_ref[rows, :] + branch_scale * acc
        xo_ref[rows, :] = x_new
        xn_ref[rows, :] = (x_new * _rms_scale(x_new) * gain).astype(xn_ref.dtype)


def _proj_residual_norm(y, w, lead, x, next_gain, branch_scale, norm_dtype, block_rows):
    s, k = y.shape
    d = w.shape[-1]
    est = _nbytes((k, d), BF16) + 2 * _nbytes((block_rows, k), BF16) \
        + 4 * _nbytes((block_rows, d), F32) + 2 * _nbytes((block_rows, d), norm_dtype) \
        + 3 * _nbytes((block_rows, d), F32)
    row_spec = lambda cols: pl.BlockSpec((block_rows, cols), lambda i: (i, 0))
    w_spec = pl.BlockSpec((None,) * len(lead) + (k, d), lambda i: tuple(lead) + (0, 0),
                          pipeline_mode=pl.Buffered(1))
    return pl.pallas_call(
        functools.partial(_proj_residual_norm_kernel, branch_scale=branch_scale),
        grid=(s // block_rows,),
        in_specs=[row_spec(k), w_spec, row_spec(d), pl.BlockSpec((1, d), lambda i: (0, 0))],
        out_specs=[row_spec(d), row_spec(d)],
        out_shape=[jax.ShapeDtypeStruct((s, d), F32), jax.ShapeDtypeStruct((s, d), norm_dtype)],
        name=f"proj_residual_norm_k{k}",
        compiler_params=_compiler_params(("parallel",), est),
    )(y, w, x, next_gain.reshape(1, d))


KV_CHUNK = 512
QKV_COLS_PER_DOT = 512


def _swap_rotary_halves(x):
    lane = lax.broadcasted_iota(jnp.int32, x.shape, 1)
    first_quarter = (lane % (2 * ROPE_FREQS)) < ROPE_FREQS
    return jnp.where(first_quarter,
                     pltpu.roll(x, HEAD_DIM - ROPE_FREQS, 1),
                     pltpu.roll(x, ROPE_FREQS, 1))


def _qkv_kernel(x_ref, w_ref, gain_ref, cos_ref, sin_ref, q_ref, kt_ref, v_ref):
    x = x_ref[...]
    cos_t = cos_ref[...]
    sin_t = sin_ref[...]
    heads_per_dot = QKV_COLS_PER_DOT // HEAD_DIM

    def normed_rotated(acc, h, col0):
        cols = slice(col0 + h * HEAD_DIM, col0 + (h + 1) * HEAD_DIM)
        xh = acc[:, h * HEAD_DIM:(h + 1) * HEAD_DIM]
        xh = xh * _rms_scale(xh) * gain_ref[:, cols]
        return xh * cos_t + _swap_rotary_halves(xh) * sin_t

    for jb in range(QKV_DIM // QKV_COLS_PER_DOT):
        col0 = jb * QKV_COLS_PER_DOT
        acc = jnp.dot(x, w_ref[:, col0:col0 + QKV_COLS_PER_DOT], preferred_element_type=F32)
        if col0 < Q_DIM:
            for h in range(heads_per_dot):
                q_ref[:, col0 + h * HEAD_DIM:col0 + (h + 1) * HEAD_DIM] = \
                    normed_rotated(acc, h, col0).astype(q_ref.dtype)
        elif col0 < Q_DIM + KV_DIM:
            for h in range(heads_per_dot):
                row0 = col0 - Q_DIM + h * HEAD_DIM
                kt_ref[row0:row0 + HEAD_DIM, :] = \
                    normed_rotated(acc, h, col0).T.astype(kt_ref.dtype)
        else:
            v0 = col0 - Q_DIM - KV_DIM
            v_ref[:, v0:v0 + QKV_COLS_PER_DOT] = acc.astype(v_ref.dtype)


def _qkv_proj(xn, w_qkv, lead, head_gain_cols, cos_t, sin_t):
    s, k = xn.shape
    rows = KV_CHUNK
    est = _nbytes((k, QKV_DIM), BF16) + 2 * _nbytes((rows, k), BF16) \
        + 4 * _nbytes((rows, HEAD_DIM), F32) + 2 * _nbytes((rows, QKV_DIM), BF16) \
        + 6 * _nbytes((rows, QKV_COLS_PER_DOT), F32)
    return pl.pallas_call(
        _qkv_kernel,
        grid=(s // rows,),
        in_specs=[pl.BlockSpec((rows, k), lambda i: (i, 0)),
                  pl.BlockSpec((None,) * len(lead) + (k, QKV_DIM),
                               lambda i: tuple(lead) + (0, 0), pipeline_mode=pl.Buffered(1)),
                  pl.BlockSpec((1, QKV_DIM), lambda i: (0, 0)),
                  pl.BlockSpec((rows, HEAD_DIM), lambda i: (i, 0)),
                  pl.BlockSpec((rows, HEAD_DIM), lambda i: (i, 0))],
        out_specs=[pl.BlockSpec((rows, Q_DIM), lambda i: (i, 0)),
                   pl.BlockSpec((None, KV_DIM, rows), lambda i: (i, 0, 0)),
                   pl.BlockSpec((rows, KV_DIM), lambda i: (i, 0))],
        out_shape=[jax.ShapeDtypeStruct((s, Q_DIM), BF16),
                   jax.ShapeDtypeStruct((s // rows, KV_DIM, rows), BF16),
                   jax.ShapeDtypeStruct((s, KV_DIM), BF16)],
        name="qkv_proj",
        compiler_params=_compiler_params(("parallel",), est),
    )(xn, w_qkv, head_gain_cols, cos_t, sin_t)


def _rotary_tables(s):
    n_rows = s // GRID_W
    inv_freq = ROPE_THETA ** (-jnp.arange(ROPE_FREQS, dtype=F32) / ROPE_FREQS)
    ang_row = jnp.arange(n_rows, dtype=F32)[:, None] * inv_freq[None, :]
    ang_col = jnp.arange(GRID_W, dtype=F32)[:, None] * inv_freq[None, :]
    by_row = lambda t: jnp.repeat(t, GRID_W, axis=0)
    by_col = lambda t: jnp.tile(t, (n_rows, 1))
    cos_r, sin_r = by_row(jnp.cos(ang_row)), by_row(jnp.sin(ang_row))
    cos_c, sin_c = by_col(jnp.cos(ang_col)), by_col(jnp.sin(ang_col))
    cos_t = jnp.concatenate([cos_r, cos_r, cos_c, cos_c], axis=-1)
    sin_t = jnp.concatenate([-sin_r, sin_r, -sin_c, sin_c], axis=-1)
    return cos_t, sin_t


FLASH_CHUNKS_PER_STEP = 8


def _flash_kernel(q_ref, kt_ref, v_ref, o_ref, q_all, s_ref, m_ref, acc_ref):
    tq = q_ref.shape[0]
    n_chunks = kt_ref.shape[0]
    lane_tiles = KV_CHUNK // LANES

    for g in range(GROUP):
        q_all[g * tq:(g + 1) * tq, :] = q_ref[:, g * HEAD_DIM:(g + 1) * HEAD_DIM]

    def scores(c):
        return jnp.dot(q_all[...], kt_ref[c], preferred_element_type=F32)

    def absorb(c, s, m, acc):
        tiles = [s[:, t * LANES:(t + 1) * LANES] for t in range(lane_tiles)]
        tile_max = functools.reduce(jnp.maximum, tiles)
        m_new = jnp.maximum(m, jnp.max(tile_max, axis=-1, keepdims=True))
        alpha = jnp.exp2(m - m_new)
        p = jnp.concatenate([jnp.exp2(t - m_new) for t in tiles], axis=1).astype(BF16)
        start = pl.multiple_of(c * KV_CHUNK, KV_CHUNK)
        v_c = v_ref[pl.ds(start, KV_CHUNK), :]
        v_ones = jnp.concatenate([v_c, jnp.ones_like(v_c)], axis=1)
        pv = jnp.dot(p, v_ones, preferred_element_type=F32)
        return m_new, jnp.concatenate([alpha, alpha], axis=1) * acc + pv

    s_ref[...] = scores(0)
    m_ref[...] = jnp.full(m_ref.shape, -jnp.inf, F32)
    acc_ref[...] = jnp.zeros(acc_ref.shape, F32)

    def chunk_group(c0, is_last):
        state = (m_ref[...], acc_ref[...])
        s_cur = s_ref[...]
        for u in range(FLASH_CHUNKS_PER_STEP):
            final_chunk = is_last and u == FLASH_CHUNKS_PER_STEP - 1
            s_next = None if final_chunk else scores(c0 + u + 1)
            state = absorb(c0 + u, s_cur, *state)
            s_cur = s_next
        m_ref[...], acc_ref[...] = state
        if not is_last:
            s_ref[...] = s_cur

    n_groups = n_chunks // FLASH_CHUNKS_PER_STEP

    def group_body(it, _):
        chunk_group(it * FLASH_CHUNKS_PER_STEP, False)
        return 0

    lax.fori_loop(0, n_groups - 1, group_body, 0)
    chunk_group((n_groups - 1) * FLASH_CHUNKS_PER_STEP, True)

    for g in range(GROUP):
        r = slice(g * tq, (g + 1) * tq)
        o_ref[:, g * HEAD_DIM:(g + 1) * HEAD_DIM] = \
            (acc_ref[r, :HEAD_DIM] / acc_ref[r, HEAD_DIM:]).astype(o_ref.dtype)


def _flash_attention(q, kt, v, block_q=256):
    s = q.shape[0]
    n_chunks = s // KV_CHUNK
    assert n_chunks % FLASH_CHUNKS_PER_STEP == 0
    group_cols = GROUP * HEAD_DIM
    rows = GROUP * block_q
    stat = pltpu.VMEM((rows, LANES), F32)
    est = 4 * _nbytes((block_q, group_cols), BF16) + 4 * _nbytes((s, HEAD_DIM), BF16) \
        + _nbytes((rows, HEAD_DIM), BF16) + _nbytes((rows, KV_CHUNK), F32) \
        + 3 * _nbytes((rows, LANES), F32) + 8 * _nbytes((rows, KV_CHUNK), F32)
    return pl.pallas_call(
        _flash_kernel,
        grid=(N_KV_HEADS, s // block_q),
        in_specs=[pl.BlockSpec((block_q, group_cols), lambda h, i: (i, h)),
                  pl.BlockSpec((n_chunks, HEAD_DIM, KV_CHUNK), lambda h, i: (0, h, 0)),
                  pl.BlockSpec((s, HEAD_DIM), lambda h, i: (0, h))],
        out_specs=pl.BlockSpec((block_q, group_cols), lambda h, i: (i, h)),
        out_shape=jax.ShapeDtypeStruct((s, Q_DIM), BF16),
        scratch_shapes=[pltpu.VMEM((rows, HEAD_DIM), BF16),
                        pltpu.VMEM((rows, KV_CHUNK), F32), stat,
                        pltpu.VMEM((rows, 2 * HEAD_DIM), F32)],
        name="flash_attention",
        compiler_params=_compiler_params(("parallel", "parallel"), est),
    )(q, kt, v)


RGLRU_CHUNK = 512
SCAN_UNROLL = 8
SCAN_BLOCK = SCAN_UNROLL * SUBLANES
CONV_HALO = SUBLANES
N_SEGMENTS = SUBLANES


def _rglru_kernel(rec_ref, gate_ref, cw_ref, cb_ref, gw_ref, gb_ref, lam_ref, y_ref,
                  xpad_ref, af_ref, bf_ref, ab_ref, bb_ref):
    s_len = rec_ref.shape[0]
    seg_len = s_len // N_SEGMENTS
    n_chunks = s_len // RGLRU_CHUNK

    def scan_rows(start):
        seg = start // seg_len
        pos = start - seg * seg_len
        return seg, pl.ds(pos * N_SEGMENTS + seg, RGLRU_CHUNK, stride=N_SEGMENTS)

    zeros_halo = jnp.zeros((CONV_HALO, LANES), F32)
    xpad_ref[0:CONV_HALO, :] = zeros_halo
    xpad_ref[CONV_HALO + s_len:CONV_HALO + s_len + CONV_HALO, :] = zeros_halo
    xpad_ref[CONV_HALO:CONV_HALO + s_len, :] = rec_ref[...]

    neg_lam = -lam_ref[...]
    softplus = jnp.maximum(neg_lam, 0.0) + jnp.log(1.0 + jnp.exp(-jnp.abs(neg_lam)))
    half_decay = (-0.5 * C_DECAY * LOG2_E) * softplus

    def gates_body(c, _):
        start = pl.multiple_of(c * RGLRU_CHUNK, RGLRU_CHUNK)
        _, rows = scan_rows(start)
        xc = cb_ref[...]
        for tap in range(CONV_W):
            off = CONV_HALO - CONV_PAD_L + tap
            xc = xc + xpad_ref[pl.ds(start + off, RGLRU_CHUNK), :] * cw_ref[tap:tap + 1, :]
        t = jnp.tanh(jnp.dot(xc.astype(BF16), gw_ref[...], preferred_element_type=F32)
                     + gb_ref[...])
        xc_half = 0.5 * xc
        for direction, (a_ref, b_ref) in enumerate(((af_ref, bf_ref), (ab_ref, bb_ref))):
            t_r = t[:, (2 * direction) * LANES:(2 * direction + 1) * LANES]
            t_i = t[:, (2 * direction + 1) * LANES:(2 * direction + 2) * LANES]
            hd = half_decay[direction:direction + 1, :]
            a = jnp.exp2(hd * t_r + hd)
            a_ref[rows, :] = a
            b_ref[rows, :] = jnp.sqrt(1.0 - a * a) * (xc_half + xc_half * t_i)
        return 0

    lax.fori_loop(0, n_chunks, gates_body, 0)

    n_iters = s_len // SCAN_BLOCK

    def scan_block(a_ref, b_ref, start, h, prod, reverse):
        rows = pl.ds(start, SCAN_BLOCK)
        a_blk = a_ref[rows, :]
        b_blk = b_ref[rows, :]
        order = range(SCAN_UNROLL - 1, -1, -1) if reverse else range(SCAN_UNROLL)
        h_out = [None] * SCAN_UNROLL
        prod_out = [None] * SCAN_UNROLL
        for u in order:
            a8 = a_blk[u * SUBLANES:(u + 1) * SUBLANES]
            h = a8 * h + b_blk[u * SUBLANES:(u + 1) * SUBLANES]
            prod = a8 * prod
            h_out[u] = h
            prod_out[u] = prod
        b_ref[rows, :] = jnp.concatenate(h_out, axis=0)
        a_ref[rows, :] = jnp.concatenate(prod_out, axis=0)
        return h, prod

    def scan_body(it, carry):
        h_f, p_f, h_b, p_b = carry
        start_f = pl.multiple_of(it * SCAN_BLOCK, SCAN_BLOCK)
        start_b = pl.multiple_of((n_iters - 1 - it) * SCAN_BLOCK, SCAN_BLOCK)
        h_f, p_f = scan_block(af_ref, bf_ref, start_f, h_f, p_f, reverse=False)
        h_b, p_b = scan_block(ab_ref, bb_ref, start_b, h_b, p_b, reverse=True)
        return h_f, p_f, h_b, p_b

    zeros = jnp.zeros((N_SEGMENTS, LANES), F32)
    ones = jnp.ones((N_SEGMENTS, LANES), F32)
    end_f, prod_f, end_b, prod_b = lax.fori_loop(0, n_iters, scan_body, (zeros, ones, zeros, ones))

    def entering_states(end, prod, order):
        state = jnp.zeros((1, LANES), F32)
        entering = [None] * N_SEGMENTS
        for seg in order:
            entering[seg] = state
            state = prod[seg:seg + 1, :] * state + end[seg:seg + 1, :]
        return jnp.concatenate(entering, axis=0)

    enter_f = jnp.tile(entering_states(end_f, prod_f, range(N_SEGMENTS)), (SCAN_UNROLL, 1))
    enter_b = jnp.tile(entering_states(end_b, prod_b, range(N_SEGMENTS - 1, -1, -1)),
                       (SCAN_UNROLL, 1))

    def resolve_body(it, _):
        rows = pl.ds(pl.multiple_of(it * SCAN_BLOCK, SCAN_BLOCK), SCAN_BLOCK)
        bf_ref[rows, :] = (bf_ref[rows, :] + af_ref[rows, :] * enter_f) \
            + (bb_ref[rows, :] + ab_ref[rows, :] * enter_b)
        return 0

    lax.fori_loop(0, n_iters, resolve_body, 0)

    def out_body(c, _):
        start = pl.multiple_of(c * RGLRU_CHUNK, RGLRU_CHUNK)
        _, rows = scan_rows(start)
        y_ref[pl.ds(start, RGLRU_CHUNK), :] = \
            (gate_ref[pl.ds(start, RGLRU_CHUNK), :] * bf_ref[rows, :]).astype(y_ref.dtype)
        return 0

    lax.fori_loop(0, n_chunks, out_body, 0)


def _rglru(rec, gelu_gate, conv_w, conv_b, gate_w_half, gate_b_half, lam):
    s = rec.shape[0]
    col_spec = lambda rows: pl.BlockSpec((rows, LANES), lambda j: (0, j))
    seq_f32 = _nbytes((s, LANES), F32)
    est = 4 * seq_f32 + 2 * _nbytes((s, LANES), BF16) + 5 * seq_f32 \
        + 12 * _nbytes((RGLRU_CHUNK, 4 * LANES), F32)
    return pl.pallas_call(
        _rglru_kernel,
        grid=(N_RNN_BLOCKS,),
        in_specs=[col_spec(s), col_spec(s), col_spec(CONV_W), col_spec(1),
                  pl.BlockSpec((None, RNN_BLOCK, 4 * RNN_BLOCK), lambda j: (j, 0, 0)),
                  pl.BlockSpec((None, 1, 4 * RNN_BLOCK), lambda j: (j, 0, 0)),
                  col_spec(2)],
        out_specs=col_spec(s),
        out_shape=jax.ShapeDtypeStruct((s, D_RNN), BF16),
        scratch_shapes=[pltpu.VMEM((s + 2 * CONV_HALO, LANES), F32)]
        + [pltpu.VMEM((s, LANES), F32)] * 4,
        name="rglru",
        compiler_params=_compiler_params(("parallel",), est),
    )(rec, gelu_gate, conv_w, conv_b.reshape(1, D_RNN), gate_w_half, gate_b_half, lam)


def kernel(x, ffn_norm, ffn_w_gu, ffn_w_down, attn_norm, attn_w_qkv, attn_q_norm, attn_k_norm,
           attn_w_o, rec_norm, rec_w_in, rec_conv_w, rec_conv_b, rec_gate_w, rec_gate_b,
           rec_lambda, rec_w_out, final_norm):
    b, s, d = x.shape
    assert (b, d) == (1, D_MODEL) and s % 1024 == 0
    x = x.reshape(s, d)
    cos_t, sin_t = _rotary_tables(s)
    w_qkv = attn_w_qkv.astype(BF16)
    w_o = attn_w_o.astype(BF16)

    def ffn(x, xn, layer, half, next_gain, norm_dtype):
        h, w_down = _dual_matmul(xn, ffn_w_gu, (layer, half), _swiglu_epilogue, (BF16,),
                                 "ffn_gate_up", sides=[(ffn_w_down, (layer, half))],
                                 block_rows=2048)
        block_rows = 512 if norm_dtype == BF16 else 256
        return _proj_residual_norm(h, w_down, (), x, next_gain, 0.5, norm_dtype, block_rows)

    xn = _rms_norm(x, ffn_norm[0, 0], BF16)
    for i in range(DEPTH):
        j = i // N_MIXERS
        is_attn = i % N_MIXERS == 0
        x, xn = ffn(x, xn, i, 0, attn_norm[j] if is_attn else rec_norm[j], BF16)
        if is_attn:
            q_scale = (HEAD_DIM ** -0.5) * LOG2_E
            head_gain_cols = jnp.concatenate(
                [jnp.tile(attn_q_norm[j] * q_scale, N_HEADS),
                 jnp.tile(attn_k_norm[j], N_KV_HEADS),
                 jnp.ones((KV_DIM,), F32)]).reshape(1, QKV_DIM)
            q, kt, v = _qkv_proj(xn, w_qkv, (j,), head_gain_cols, cos_t, sin_t)
            y = _flash_attention(q, kt, v)
            w_mix, lead = w_o, (j,)
        else:
            gelu_gate, rec, w_out = _dual_matmul(xn, rec_w_in, (j,), _gelu_split_epilogue,
                                                 (F32, F32), "rec_in_proj",
                                                 sides=[(rec_w_out, (j,))])
            gate_w_half = (0.5 * jnp.transpose(rec_gate_w[j], (2, 3, 0, 1, 4))).reshape(
                N_RNN_BLOCKS, RNN_BLOCK, 4 * RNN_BLOCK).astype(BF16)
            gate_b_half = 0.5 * jnp.transpose(
                rec_gate_b[j].reshape(2, 2, N_RNN_BLOCKS, RNN_BLOCK), (2, 0, 1, 3)).reshape(
                N_RNN_BLOCKS, 1, 4 * RNN_BLOCK)
            y = _rglru(rec, gelu_gate, rec_conv_w[j], rec_conv_b[j], gate_w_half, gate_b_half,
                       rec_lambda[j])
            w_mix, lead = w_out, ()
        x, xn = _proj_residual_norm(y, w_mix, lead, x, ffn_norm[i, 1], 1.0, BF16, block_rows=512)
        last = i == DEPTH - 1
        next_gain = final_norm if last else ffn_norm[i + 1, 0]
        x, xn = ffn(x, xn, i, 1, next_gain, F32 if last else BF16)
    return xn.reshape(b, s, d)
```

```python
import functools
import math

import jax
import jax.numpy as jnp
import numpy as np
from jax import lax
from jax.experimental import pallas as pl
from jax.experimental.pallas import tpu as pltpu

D_MODEL = 2048
DEPTH = 4
N_MIXERS = 2
GRID_W = 64
HEAD_DIM = 128
N_HEADS = D_MODEL // HEAD_DIM
N_KV_HEADS = 4
GROUP = N_HEADS // N_KV_HEADS
Q_DIM = N_HEADS * HEAD_DIM
KV_DIM = N_KV_HEADS * HEAD_DIM
QKV_DIM = Q_DIM + 2 * KV_DIM
ROPE_THETA = 10000.0
ROPE_FREQS = HEAD_DIM // 4
D_RNN = D_MODEL
RNN_BLOCK = 128
N_RNN_BLOCKS = D_RNN // RNN_BLOCK
CONV_W = 4
CONV_PAD_L = 2
C_DECAY = 8.0
D_FF = 5632
NORM_EPS = 1e-6

V7X_VMEM_USABLE_BYTES = 58 * 1024 * 1024
SUBLANES = 8
LANES = 128

F32 = jnp.float32
BF16 = jnp.bfloat16
LOG2_E = math.log2(math.e)


def _nbytes(shape, dtype):
    return int(np.prod(shape)) * jnp.dtype(dtype).itemsize


def _compiler_params(semantics, vmem_estimate_bytes):
    limit = min(V7X_VMEM_USABLE_BYTES, max(32 * 1024 * 1024, int(vmem_estimate_bytes)))
    return pltpu.CompilerParams(dimension_semantics=semantics, vmem_limit_bytes=limit)


def _rms_scale(x):
    return lax.rsqrt(jnp.mean(x * x, axis=-1, keepdims=True) + NORM_EPS)


def _stacked_spec(block_shape, lead, index_map):
    return pl.BlockSpec((None,) * len(lead) + tuple(block_shape),
                        lambda *g: tuple(lead) + tuple(index_map(*g)))


def _rms_norm_kernel(x_ref, g_ref, o_ref):
    x = x_ref[...]
    o_ref[...] = (x * _rms_scale(x) * g_ref[...]).astype(o_ref.dtype)


def _rms_norm(x, gain, out_dtype, block_rows=512):
    s, d = x.shape
    est = 2 * (_nbytes((block_rows, d), F32) + _nbytes((block_rows, d), out_dtype)) \
        + 2 * _nbytes((block_rows, d), F32)
    return pl.pallas_call(
        _rms_norm_kernel,
        grid=(s // block_rows,),
        in_specs=[pl.BlockSpec((block_rows, d), lambda i: (i, 0)),
                  pl.BlockSpec((1, d), lambda i: (0, 0))],
        out_specs=pl.BlockSpec((block_rows, d), lambda i: (i, 0)),
        out_shape=jax.ShapeDtypeStruct((s, d), out_dtype),
        name="rms_norm",
        compiler_params=_compiler_params(("parallel",), est),
    )(x, gain.reshape(1, d))


def _swiglu_epilogue(a, b):
    return ((a / (1.0 + jnp.exp(-a))) * b,)


def _gelu_split_epilogue(a, b):
    c = math.sqrt(2.0 / math.pi)
    cdf = 0.5 * (1.0 + jnp.tanh(c * (a + 0.044715 * (a * a * a))))
    return (a * cdf, b)


DUAL_MATMUL_SUB_ROWS = 512


def _dual_matmul_kernel(x_ref, wa_ref, wb_ref, *refs, epilogue, n_sides):
    side_refs = refs[:n_sides]
    out_refs = refs[n_sides:len(refs) - n_sides]
    side_out_refs = refs[len(refs) - n_sides:]
    for side_ref, side_out_ref in zip(side_refs, side_out_refs):
        side_out_ref[...] = side_ref[...].astype(side_out_ref.dtype)
    wa = wa_ref[...].astype(BF16)
    wb = wb_ref[...].astype(BF16)
    for r0 in range(0, x_ref.shape[0], DUAL_MATMUL_SUB_ROWS):
        rows = slice(r0, r0 + DUAL_MATMUL_SUB_ROWS)
        x = x_ref[rows, :]
        a = jnp.dot(x, wa, preferred_element_type=F32)
        b = jnp.dot(x, wb, preferred_element_type=F32)
        for ref, val in zip(out_refs, epilogue(a, b)):
            ref[rows, :] = val.astype(ref.dtype)


def _dual_matmul(x, w, lead, epilogue, out_dtypes, name, sides=(), block_rows=1024,
                 block_cols=512):
    s, k = x.shape
    n = w.shape[-1] // 2
    n_col_blocks = n // block_cols
    n_row_blocks = s // block_rows
    est = 2 * _nbytes((block_rows, k), BF16) + 4 * _nbytes((k, block_cols), w.dtype) \
        + 2 * _nbytes((k, block_cols), BF16) \
        + sum(2 * _nbytes((block_rows, block_cols), dt) for dt in out_dtypes) \
        + 4 * _nbytes((block_rows, block_cols), F32)
    out_spec = pl.BlockSpec((block_rows, block_cols), lambda i, j: (i, j))
    in_specs = [pl.BlockSpec((block_rows, k), lambda i, j: (i, 0)),
                _stacked_spec((k, block_cols), lead, lambda i, j: (0, j)),
                _stacked_spec((k, block_cols), lead, lambda i, j: (0, j + n_col_blocks))]
    out_specs = [out_spec] * len(out_dtypes)
    out_shape = [jax.ShapeDtypeStruct((s, n), dt) for dt in out_dtypes]
    operands = [x, w, w]
    slab_index = lambda i, j: (i * n_col_blocks + j, 0)
    for side_w, side_lead in sides:
        side_rows, side_cols = side_w.shape[-2:]
        slab, rem = divmod(side_rows, n_row_blocks * n_col_blocks)
        assert rem == 0 and slab % (2 * SUBLANES) == 0
        in_specs.append(_stacked_spec((slab, side_cols), side_lead, slab_index))
        out_specs.append(pl.BlockSpec((slab, side_cols), slab_index))
        out_shape.append(jax.ShapeDtypeStruct((side_rows, side_cols), BF16))
        operands.append(side_w)
        est += 2 * _nbytes((slab, side_cols), F32) + 2 * _nbytes((slab, side_cols), BF16)
    return pl.pallas_call(
        functools.partial(_dual_matmul_kernel, epilogue=epilogue, n_sides=len(sides)),
        grid=(n_row_blocks, n_col_blocks),
        in_specs=in_specs,
        out_specs=out_specs,
        out_shape=out_shape,
        name=name,
        compiler_params=_compiler_params(("parallel", "arbitrary"), est),
    )(*operands)


def _proj_residual_norm_kernel(y_ref, w_ref, x_ref, g_ref, *refs, branch_scale, n_sides):
    side_refs = refs[:n_sides]
    xo_ref, xn_ref = refs[n_sides:n_sides + 2]
    side_out_refs = refs[n_sides + 2:]
    for side_ref, side_out_ref in zip(side_refs, side_out_refs):
        side_out_ref[...] = side_ref[...].astype(side_out_ref.dtype)
    acc = jnp.dot(y_ref[...], w_ref[...], preferred_element_type=F32)
    x_new = x_ref[...] + branch_scale * acc
    xo_ref[...] = x_new
    xn_ref[...] = (x_new * _rms_scale(x_new) * g_ref[...]).astype(xn_ref.dtype)


def _proj_residual_norm(y, w, lead, x, next_gain, branch_scale, norm_dtype, block_rows, sides=()):
    s, k = y.shape
    d = w.shape[-1]
    n_steps = s // block_rows
    est = _nbytes((k, d), BF16) + 2 * _nbytes((block_rows, k), BF16) \
        + 4 * _nbytes((block_rows, d), F32) + 2 * _nbytes((block_rows, d), norm_dtype) \
        + 3 * _nbytes((block_rows, d), F32)
    row_spec = lambda cols: pl.BlockSpec((block_rows, cols), lambda i: (i, 0))
    w_spec = pl.BlockSpec((None,) * len(lead) + (k, d), lambda i: tuple(lead) + (0, 0),
                          pipeline_mode=pl.Buffered(1))
    in_specs = [row_spec(k), w_spec, row_spec(d), pl.BlockSpec((1, d), lambda i: (0, 0))]
    out_specs = [row_spec(d), row_spec(d)]
    out_shape = [jax.ShapeDtypeStruct((s, d), F32), jax.ShapeDtypeStruct((s, d), norm_dtype)]
    operands = [y, w, x, next_gain.reshape(1, d)]
    for side_w, side_lead in sides:
        side_rows, side_cols = side_w.shape[-2:]
        slab, rem = divmod(side_rows, n_steps)
        assert rem == 0 and slab % (2 * SUBLANES) == 0
        in_specs.append(_stacked_spec((slab, side_cols), side_lead, lambda i: (i, 0)))
        out_specs.append(pl.BlockSpec((slab, side_cols), lambda i: (i, 0)))
        out_shape.append(jax.ShapeDtypeStruct((side_rows, side_cols), BF16))
        operands.append(side_w)
        est += 2 * _nbytes((slab, side_cols), F32) + 2 * _nbytes((slab, side_cols), BF16)
    return pl.pallas_call(
        functools.partial(_proj_residual_norm_kernel, branch_scale=branch_scale,
                          n_sides=len(sides)),
        grid=(n_steps,),
        in_specs=in_specs,
        out_specs=out_specs,
        out_shape=out_shape,
        name=f"proj_residual_norm_k{k}",
        compiler_params=_compiler_params(("parallel",), est),
    )(*operands)


KV_CHUNK = 512
QKV_COLS_PER_DOT = 512


def _swap_rotary_halves(x):
    lane = lax.broadcasted_iota(jnp.int32, x.shape, 1)
    first_quarter = (lane % (2 * ROPE_FREQS)) < ROPE_FREQS
    return jnp.where(first_quarter,
                     pltpu.roll(x, HEAD_DIM - ROPE_FREQS, 1),
                     pltpu.roll(x, ROPE_FREQS, 1))


def _qkv_kernel(x_ref, w_ref, gain_ref, cos_ref, sin_ref, q_ref, kt_ref, v_ref):
    x = x_ref[...]
    cos_t = cos_ref[...]
    sin_t = sin_ref[...]
    heads_per_dot = QKV_COLS_PER_DOT // HEAD_DIM

    def normed_rotated(acc, h, col0):
        cols = slice(col0 + h * HEAD_DIM, col0 + (h + 1) * HEAD_DIM)
        xh = acc[:, h * HEAD_DIM:(h + 1) * HEAD_DIM]
        xh = xh * _rms_scale(xh) * gain_ref[:, cols]
        return xh * cos_t + _swap_rotary_halves(xh) * sin_t

    for jb in range(QKV_DIM // QKV_COLS_PER_DOT):
        col0 = jb * QKV_COLS_PER_DOT
        acc = jnp.dot(x, w_ref[:, col0:col0 + QKV_COLS_PER_DOT], preferred_element_type=F32)
        if col0 < Q_DIM:
            for h in range(heads_per_dot):
                q_ref[:, col0 + h * HEAD_DIM:col0 + (h + 1) * HEAD_DIM] = \
                    normed_rotated(acc, h, col0).astype(q_ref.dtype)
        elif col0 < Q_DIM + KV_DIM:
            for h in range(heads_per_dot):
                row0 = col0 - Q_DIM + h * HEAD_DIM
                kt_ref[row0:row0 + HEAD_DIM, :] = \
                    normed_rotated(acc, h, col0).T.astype(kt_ref.dtype)
        else:
            v0 = col0 - Q_DIM - KV_DIM
            v_ref[:, v0:v0 + QKV_COLS_PER_DOT] = acc.astype(v_ref.dtype)


def _qkv_proj(xn, w_qkv, lead, head_gain_cols, cos_t, sin_t):
    s, k = xn.shape
    rows = KV_CHUNK
    est = _nbytes((k, QKV_DIM), BF16) + 2 * _nbytes((rows, k), BF16) \
        + 4 * _nbytes((rows, HEAD_DIM), F32) + 2 * _nbytes((rows, QKV_DIM), BF16) \
        + 6 * _nbytes((rows, QKV_COLS_PER_DOT), F32)
    return pl.pallas_call(
        _qkv_kernel,
        grid=(s // rows,),
        in_specs=[pl.BlockSpec((rows, k), lambda i: (i, 0)),
                  pl.BlockSpec((None,) * len(lead) + (k, QKV_DIM),
                               lambda i: tuple(lead) + (0, 0), pipeline_mode=pl.Buffered(1)),
                  pl.BlockSpec((1, QKV_DIM), lambda i: (0, 0)),
                  pl.BlockSpec((rows, HEAD_DIM), lambda i: (i, 0)),
                  pl.BlockSpec((rows, HEAD_DIM), lambda i: (i, 0))],
        out_specs=[pl.BlockSpec((rows, Q_DIM), lambda i: (i, 0)),
                   pl.BlockSpec((None, KV_DIM, rows), lambda i: (i, 0, 0)),
                   pl.BlockSpec((rows, KV_DIM), lambda i: (i, 0))],
        out_shape=[jax.ShapeDtypeStruct((s, Q_DIM), BF16),
                   jax.ShapeDtypeStruct((s // rows, KV_DIM, rows), BF16),
                   jax.ShapeDtypeStruct((s, KV_DIM), BF16)],
        name="qkv_proj",
        compiler_params=_compiler_params(("parallel",), est),
    )(xn, w_qkv, head_gain_cols, cos_t, sin_t)


def _rotary_tables(s):
    n_rows = s // GRID_W
    inv_freq = ROPE_THETA ** (-jnp.arange(ROPE_FREQS, dtype=F32) / ROPE_FREQS)
    ang_row = jnp.arange(n_rows, dtype=F32)[:, None] * inv_freq[None, :]
    ang_col = jnp.arange(GRID_W, dtype=F32)[:, None] * inv_freq[None, :]
    by_row = lambda t: jnp.repeat(t, GRID_W, axis=0)
    by_col = lambda t: jnp.tile(t, (n_rows, 1))
    cos_r, sin_r = by_row(jnp.cos(ang_row)), by_row(jnp.sin(ang_row))
    cos_c, sin_c = by_col(jnp.cos(ang_col)), by_col(jnp.sin(ang_col))
    cos_t = jnp.concatenate([cos_r, cos_r, cos_c, cos_c], axis=-1)
    sin_t = jnp.concatenate([-sin_r, sin_r, -sin_c, sin_c], axis=-1)
    return cos_t, sin_t


FLASH_CHUNKS_PER_STEP = 8


def _flash_kernel(q_ref, kt_ref, v_ref, o_ref, q_all, s_ref, m_ref, acc_ref):
    tq = q_ref.shape[0]
    n_chunks = kt_ref.shape[0]
    lane_tiles = KV_CHUNK // LANES

    for g in range(GROUP):
        q_all[g * tq:(g + 1) * tq, :] = q_ref[:, g * HEAD_DIM:(g + 1) * HEAD_DIM]

    def scores(c):
        return jnp.dot(q_all[...], kt_ref[c], preferred_element_type=F32)

    def absorb(c, s, m, acc):
        tiles = [s[:, t * LANES:(t + 1) * LANES] for t in range(lane_tiles)]
        tile_max = functools.reduce(jnp.maximum, tiles)
        m_new = jnp.maximum(m, jnp.max(tile_max, axis=-1, keepdims=True))
        alpha = jnp.exp2(m - m_new)
        p = jnp.concatenate([jnp.exp2(t - m_new) for t in tiles], axis=1).astype(BF16)
        start = pl.multiple_of(c * KV_CHUNK, KV_CHUNK)
        v_c = v_ref[pl.ds(start, KV_CHUNK), :]
        v_ones = jnp.concatenate([v_c, jnp.ones_like(v_c)], axis=1)
        pv = jnp.dot(p, v_ones, preferred_element_type=F32)
        return m_new, jnp.concatenate([alpha, alpha], axis=1) * acc + pv

    s_ref[...] = scores(0)
    m_ref[...] = jnp.full(m_ref.shape, -jnp.inf, F32)
    acc_ref[...] = jnp.zeros(acc_ref.shape, F32)

    def chunk_group(c0, is_last):
        state = (m_ref[...], acc_ref[...])
        s_cur = s_ref[...]
        for u in range(FLASH_CHUNKS_PER_STEP):
            final_chunk = is_last and u == FLASH_CHUNKS_PER_STEP - 1
            s_next = None if final_chunk else scores(c0 + u + 1)
            state = absorb(c0 + u, s_cur, *state)
            s_cur = s_next
        m_ref[...], acc_ref[...] = state
        if not is_last:
            s_ref[...] = s_cur

    n_groups = n_chunks // FLASH_CHUNKS_PER_STEP

    def group_body(it, _):
        chunk_group(it * FLASH_CHUNKS_PER_STEP, False)
        return 0

    lax.fori_loop(0, n_groups - 1, group_body, 0)
    chunk_group((n_groups - 1) * FLASH_CHUNKS_PER_STEP, True)

    for g in range(GROUP):
        r = slice(g * tq, (g + 1) * tq)
        o_ref[:, g * HEAD_DIM:(g + 1) * HEAD_DIM] = \
            (acc_ref[r, :HEAD_DIM] / acc_ref[r, HEAD_DIM:]).astype(o_ref.dtype)


def _flash_attention(q, kt, v, block_q=256):
    s = q.shape[0]
    n_chunks = s // KV_CHUNK
    assert n_chunks % FLASH_CHUNKS_PER_STEP == 0
    group_cols = GROUP * HEAD_DIM
    rows = GROUP * block_q
    stat = pltpu.VMEM((rows, LANES), F32)
    est = 4 * _nbytes((block_q, group_cols), BF16) + 4 * _nbytes((s, HEAD_DIM), BF16) \
        + _nbytes((rows, HEAD_DIM), BF16) + _nbytes((rows, KV_CHUNK), F32) \
        + 3 * _nbytes((rows, LANES), F32) + 8 * _nbytes((rows, KV_CHUNK), F32)
    return pl.pallas_call(
        _flash_kernel,
        grid=(N_KV_HEADS, s // block_q),
        in_specs=[pl.BlockSpec((block_q, group_cols), lambda h, i: (i, h)),
                  pl.BlockSpec((n_chunks, HEAD_DIM, KV_CHUNK), lambda h, i: (0, h, 0)),
                  pl.BlockSpec((s, HEAD_DIM), lambda h, i: (0, h))],
        out_specs=pl.BlockSpec((block_q, group_cols), lambda h, i: (i, h)),
        out_shape=jax.ShapeDtypeStruct((s, Q_DIM), BF16),
        scratch_shapes=[pltpu.VMEM((rows, HEAD_DIM), BF16),
                        pltpu.VMEM((rows, KV_CHUNK), F32), stat,
                        pltpu.VMEM((rows, 2 * HEAD_DIM), F32)],
        name="flash_attention",
        compiler_params=_compiler_params(("parallel", "parallel"), est),
    )(q, kt, v)


RGLRU_CHUNK = 512
SCAN_UNROLL = 8
SCAN_BLOCK = SCAN_UNROLL * SUBLANES
CONV_HALO = SUBLANES
N_SEGMENTS = SUBLANES


def _rglru_kernel(rec_ref, gate_ref, cw_ref, cb_ref, gw_ref, gb_ref, lam_ref, y_ref,
                  xpad_ref, af_ref, bf_ref, ab_ref, bb_ref):
    s_len = rec_ref.shape[0]
    seg_len = s_len // N_SEGMENTS
    n_chunks = s_len // RGLRU_CHUNK

    def scan_rows(start):
        seg = start // seg_len
        pos = start - seg * seg_len
        return seg, pl.ds(pos * N_SEGMENTS + seg, RGLRU_CHUNK, stride=N_SEGMENTS)

    zeros_halo = jnp.zeros((CONV_HALO, LANES), F32)
    xpad_ref[0:CONV_HALO, :] = zeros_halo
    xpad_ref[CONV_HALO + s_len:CONV_HALO + s_len + CONV_HALO, :] = zeros_halo
    xpad_ref[CONV_HALO:CONV_HALO + s_len, :] = rec_ref[...]

    neg_lam = -lam_ref[...]
    softplus = jnp.maximum(neg_lam, 0.0) + jnp.log(1.0 + jnp.exp(-jnp.abs(neg_lam)))
    half_decay = (-0.5 * C_DECAY * LOG2_E) * softplus

    def gates_body(c, _):
        start = pl.multiple_of(c * RGLRU_CHUNK, RGLRU_CHUNK)
        _, rows = scan_rows(start)
        xc = cb_ref[...]
        for tap in range(CONV_W):
            off = CONV_HALO - CONV_PAD_L + tap
            xc = xc + xpad_ref[pl.ds(start + off, RGLRU_CHUNK), :] * cw_ref[tap:tap + 1, :]
        t = jnp.tanh(jnp.dot(xc.astype(BF16), gw_ref[...], preferred_element_type=F32)
                     + gb_ref[...])
        xc_half = 0.5 * xc
        for direction, (a_ref, b_ref) in enumerate(((af_ref, bf_ref), (ab_ref, bb_ref))):
            t_r = t[:, (2 * direction) * LANES:(2 * direction + 1) * LANES]
            t_i = t[:, (2 * direction + 1) * LANES:(2 * direction + 2) * LANES]
            hd = half_decay[direction:direction + 1, :]
            a = jnp.exp2(hd * t_r + hd)
            a_ref[rows, :] = a
            b_ref[rows, :] = jnp.sqrt(1.0 - a * a) * (xc_half + xc_half * t_i)
        return 0

    lax.fori_loop(0, n_chunks, gates_body, 0)

    n_iters = s_len // SCAN_BLOCK

    def scan_block(a_ref, b_ref, start, h, prod, reverse):
        rows = pl.ds(start, SCAN_BLOCK)
        a_blk = a_ref[rows, :]
        b_blk = b_ref[rows, :]
        order = range(SCAN_UNROLL - 1, -1, -1) if reverse else range(SCAN_UNROLL)
        h_out = [None] * SCAN_UNROLL
        prod_out = [None] * SCAN_UNROLL
        for u in order:
            a8 = a_blk[u * SUBLANES:(u + 1) * SUBLANES]
            h = a8 * h + b_blk[u * SUBLANES:(u + 1) * SUBLANES]
            prod = a8 * prod
            h_out[u] = h
            prod_out[u] = prod
        b_ref[rows, :] = jnp.concatenate(h_out, axis=0)
        a_ref[rows, :] = jnp.concatenate(prod_out, axis=0)
        return h, prod

    def scan_body(it, carry):
        h_f, p_f, h_b, p_b = carry
        start_f = pl.multiple_of(it * SCAN_BLOCK, SCAN_BLOCK)
        start_b = pl.multiple_of((n_iters - 1 - it) * SCAN_BLOCK, SCAN_BLOCK)
        h_f, p_f = scan_block(af_ref, bf_ref, start_f, h_f, p_f, reverse=False)
        h_b, p_b = scan_block(ab_ref, bb_ref, start_b, h_b, p_b, reverse=True)
        return h_f, p_f, h_b, p_b

    zeros = jnp.zeros((N_SEGMENTS, LANES), F32)
    ones = jnp.ones((N_SEGMENTS, LANES), F32)
    end_f, prod_f, end_b, prod_b = lax.fori_loop(0, n_iters, scan_body, (zeros, ones, zeros, ones))

    def entering_states(end, prod, order):
        state = jnp.zeros((1, LANES), F32)
        entering = [None] * N_SEGMENTS
        for seg in order:
            entering[seg] = state
            state = prod[seg:seg + 1, :] * state + end[seg:seg + 1, :]
        return jnp.concatenate(entering, axis=0)

    enter_f = jnp.tile(entering_states(end_f, prod_f, range(N_SEGMENTS)), (SCAN_UNROLL, 1))
    enter_b = jnp.tile(entering_states(end_b, prod_b, range(N_SEGMENTS - 1, -1, -1)),
                       (SCAN_UNROLL, 1))

    def resolve_body(it, _):
        rows = pl.ds(pl.multiple_of(it * SCAN_BLOCK, SCAN_BLOCK), SCAN_BLOCK)
        bf_ref[rows, :] = (bf_ref[rows, :] + af_ref[rows, :] * enter_f) \
            + (bb_ref[rows, :] + ab_ref[rows, :] * enter_b)
        return 0

    lax.fori_loop(0, n_iters, resolve_body, 0)

    def out_body(c, _):
        start = pl.multiple_of(c * RGLRU_CHUNK, RGLRU_CHUNK)
        _, rows = scan_rows(start)
        y_ref[pl.ds(start, RGLRU_CHUNK), :] = \
            (gate_ref[pl.ds(start, RGLRU_CHUNK), :] * bf_ref[rows, :]).astype(y_ref.dtype)
        return 0

    lax.fori_loop(0, n_chunks, out_body, 0)


def _rglru(rec, gelu_gate, conv_w, conv_b, gate_w_half, gate_b_half, lam):
    s = rec.shape[0]
    col_spec = lambda rows: pl.BlockSpec((rows, LANES), lambda j: (0, j))
    seq_f32 = _nbytes((s, LANES), F32)
    est = 4 * seq_f32 + 2 * _nbytes((s, LANES), BF16) + 5 * seq_f32 \
        + 12 * _nbytes((RGLRU_CHUNK, 4 * LANES), F32)
    return pl.pallas_call(
        _rglru_kernel,
        grid=(N_RNN_BLOCKS,),
        in_specs=[col_spec(s), col_spec(s), col_spec(CONV_W), col_spec(1),
                  pl.BlockSpec((None, RNN_BLOCK, 4 * RNN_BLOCK), lambda j: (j, 0, 0)),
                  pl.BlockSpec((None, 1, 4 * RNN_BLOCK), lambda j: (j, 0, 0)),
                  col_spec(2)],
        out_specs=col_spec(s),
        out_shape=jax.ShapeDtypeStruct((s, D_RNN), BF16),
        scratch_shapes=[pltpu.VMEM((s + 2 * CONV_HALO, LANES), F32)]
        + [pltpu.VMEM((s, LANES), F32)] * 4,
        name="rglru",
        compiler_params=_compiler_params(("parallel",), est),
    )(rec, gelu_gate, conv_w, conv_b.reshape(1, D_RNN), gate_w_half, gate_b_half, lam)


def kernel(x, ffn_norm, ffn_w_gu, ffn_w_down, attn_norm, attn_w_qkv, attn_q_norm, attn_k_norm,
           attn_w_o, rec_norm, rec_w_in, rec_conv_w, rec_conv_b, rec_gate_w, rec_gate_b,
           rec_lambda, rec_w_out, final_norm):
    b, s, d = x.shape
    assert (b, d) == (1, D_MODEL) and s % 1024 == 0
    x = x.reshape(s, d)
    cos_t, sin_t = _rotary_tables(s)

    def ffn(x, xn, layer, half, next_gain, norm_dtype, down_sides=()):
        h, w_down = _dual_matmul(xn, ffn_w_gu, (layer, half), _swiglu_epilogue, (BF16,),
                                 "ffn_gate_up", sides=[(ffn_w_down, (layer, half))],
                                 block_rows=2048)
        return _proj_residual_norm(h, w_down, (), x, next_gain, 0.5, norm_dtype, block_rows=256,
                                   sides=down_sides)

    xn = _rms_norm(x, ffn_norm[0, 0], BF16)
    for i in range(DEPTH):
        j = i // N_MIXERS
        is_attn = i % N_MIXERS == 0
        if is_attn:
            x, xn, w_qkv, w_o = ffn(x, xn, i, 0, attn_norm[j], BF16,
                                    down_sides=[(attn_w_qkv, (j,)), (attn_w_o, (j,))])
            q_scale = (HEAD_DIM ** -0.5) * LOG2_E
            head_gain_cols = jnp.concatenate(
                [jnp.tile(attn_q_norm[j] * q_scale, N_HEADS),
                 jnp.tile(attn_k_norm[j], N_KV_HEADS),
                 jnp.ones((KV_DIM,), F32)]).reshape(1, QKV_DIM)
            q, kt, v = _qkv_proj(xn, w_qkv, (), head_gain_cols, cos_t, sin_t)
            y = _flash_attention(q, kt, v)
            w_mix = w_o
        else:
            x, xn = ffn(x, xn, i, 0, rec_norm[j], BF16)
            gelu_gate, rec, w_out = _dual_matmul(xn, rec_w_in, (j,), _gelu_split_epilogue,
                                                 (F32, F32), "rec_in_proj",
                                                 sides=[(rec_w_out, (j,))])
            gate_w_half = (0.5 * jnp.transpose(rec_gate_w[j], (2, 3, 0, 1, 4))).reshape(
                N_RNN_BLOCKS, RNN_BLOCK, 4 * RNN_BLOCK).astype(BF16)
            gate_b_half = 0.5 * jnp.transpose(
                rec_gate_b[j].reshape(2, 2, N_RNN_BLOCKS, RNN_BLOCK), (2, 0, 1, 3)).reshape(
                N_RNN_BLOCKS, 1, 4 * RNN_BLOCK)
            y = _rglru(rec, gelu_gate, rec_conv_w[j], rec_conv_b[j], gate_w_half, gate_b_half,
                       rec_lambda[j])
            w_mix = w_out
        x, xn = _proj_residual_norm(y, w_mix, (), x, ffn_norm[i, 1], 1.0, BF16, block_rows=512)
        last = i == DEPTH - 1
        next_gain = final_norm if last else ffn_norm[i + 1, 0]
        x, xn = ffn(x, xn, i, 1, next_gain, F32 if last else BF16)
    return xn.reshape(b, s, d)
```
